```python
import jax, jax.numpy as jnp
from jax import lax
import numpy as np

D_MODEL = 1024
BATCH = 8
SEQ = 8192
DEPTH = 1

EPS = 1e-6
RWKV_HEADS = 8
RWKV_HEAD_DIM = 64
D_RWKV = RWKV_HEADS * RWKV_HEAD_DIM
DECAY_LORA = 64
AAA_LORA = 64
GATE_LORA = 128
LN_X_EPS = 64e-5
MLA_HEADS = 8
QK_NOPE_DIM = 64
QK_ROPE_DIM = 32
QK_HEAD_DIM = QK_NOPE_DIM + QK_ROPE_DIM
V_HEAD_DIM = 64
D_MLA = MLA_HEADS * V_HEAD_DIM
Q_LORA = 256
KV_LORA = 128
ROPE_THETA = 10000.0
Q_BLOCK = 128
RWKV_COLS = 3 * D_RWKV + DECAY_LORA + AAA_LORA + GATE_LORA
MLA_COLS = Q_LORA + KV_LORA + QK_ROPE_DIM
N_BRANCHES = 2
D_IN = RWKV_COLS + MLA_COLS + N_BRANCHES * D_MODEL
D_BRANCH = 512
N_GROUPS = 4
EXPERTS_PER_GROUP = 8
N_EXPERTS = N_GROUPS * EXPERTS_PER_GROUP
TOP_K = 2
D_EXPERT = 256

kernel_name = "hybrid_rwkv7_mla_hmoe_adaln_block"


def rmsnorm(x, g):
    xf = x.astype(jnp.float32)
    y = xf * lax.rsqrt(jnp.mean(xf * xf, axis=-1, keepdims=True) + EPS)
    return (y * g.astype(jnp.float32)).astype(x.dtype)


def modulate(h, shift, scale):
    return h * (1 + scale[:, None, :]) + shift[:, None, :]


def token_shift(z):
    return jnp.pad(z, ((0, 0), (1, 0), (0, 0)))[:, :-1]


def rope(x, cos, sin):
    x1, x2 = jnp.split(x, 2, axis=-1)
    return jnp.concatenate([x1 * cos - x2 * sin, x2 * cos + x1 * sin], axis=-1)


def rwkv7_scan(r, w, k, v, kk, b):
    bsz, _, nh, n = r.shape
    xs = tuple(jnp.moveaxis(t.astype(jnp.float32), 1, 0) for t in (r, w, k, v, kk, b))

    def step(state, inp):
        r_t, w_t, k_t, v_t, kk_t, b_t = inp
        sa = jnp.einsum('bhij,bhj->bhi', state, -kk_t)
        state = (state * w_t[:, :, None, :]
                 + sa[..., None] * b_t[:, :, None, :]
                 + v_t[..., None] * k_t[:, :, None, :])
        y = jnp.einsum('bhij,bhj->bhi', state, r_t)
        return state, y

    state0 = jnp.zeros((bsz, nh, n, n), jnp.float32)
    _, ys = lax.scan(step, state0, xs)
    return jnp.moveaxis(ys, 0, 1)


def rwkv7_branch(z, mu, w0, w_decay_up, a0, w_a_up, w_g_up, k_k, k_a, r_k, ln_w, ln_b):
    bsz, s, _ = z.shape
    z = z + (token_shift(z) - z) * mu
    zr, zk, zv, zw, za, zg = jnp.split(
        z, [D_RWKV, 2 * D_RWKV, 3 * D_RWKV, 3 * D_RWKV + DECAY_LORA,
            3 * D_RWKV + DECAY_LORA + AAA_LORA], axis=-1)
    w_log = -jax.nn.softplus(-(w0 + jnp.tanh(zw) @ w_decay_up)) - 0.5
    decay = jnp.exp(-jnp.exp(w_log.astype(jnp.float32)))
    a = jax.nn.sigmoid(a0 + za @ w_a_up)
    g = jax.nn.sigmoid(zg) @ w_g_up

    heads = lambda t: t.reshape(bsz, s, RWKV_HEADS, RWKV_HEAD_DIM)
    kk = heads(zk * k_k).astype(jnp.float32)
    kk = kk / jnp.maximum(jnp.sqrt(jnp.sum(kk * kk, axis=-1, keepdims=True)), 1e-12)
    k = zk * (1 + (a - 1) * k_a)
    r_h, k_h, v_h, a_h = heads(zr), heads(k), heads(zv), heads(a)

    y = rwkv7_scan(r_h, heads(decay), k_h, v_h, kk, kk * a_h)
    mean = jnp.mean(y, axis=-1, keepdims=True)
    var = jnp.mean(jnp.square(y - mean), axis=-1, keepdims=True)
    y = ((y - mean) * lax.rsqrt(var + LN_X_EPS)).reshape(bsz, s, D_RWKV)
    y = (y * ln_w.astype(jnp.float32) + ln_b.astype(jnp.float32)).astype(z.dtype)
    bonus = jnp.sum(r_h * k_h * r_k, axis=-1, keepdims=True) * v_h
    return (y + bonus.reshape(bsz, s, D_RWKV)) * g


def causal_attention(q, k, v):
    bsz, s, nh, dv = v.shape
    q = q.transpose(0, 2, 1, 3)
    k = k.transpose(0, 2, 1, 3)
    v = v.transpose(0, 2, 1, 3)
    scale = QK_HEAD_DIM ** -0.5
    kpos = jnp.arange(s)

    def block(i):
        start = i * Q_BLOCK
        qb = lax.dynamic_slice_in_dim(q, start, Q_BLOCK, axis=2)
        sc = jnp.einsum('bhqd,bhkd->bhqk', qb, k).astype(jnp.float32) * scale
        qpos = start + jnp.arange(Q_BLOCK)
        sc = jnp.where(kpos[None, :] <= qpos[:, None], sc, -jnp.inf)
        p = jax.nn.softmax(sc, axis=-1).astype(v.dtype)
        return jnp.einsum('bhqk,bhkd->bhqd', p, v)

    o = lax.map(block, jnp.arange(s // Q_BLOCK))
    return o.transpose(1, 0, 3, 2, 4).reshape(bsz, s, nh, dv)


def mla_branch(z, cos, sin, g_q_a, w_uq, g_kv_a, w_ukv, q_norm, k_norm):
    bsz, s, _ = z.shape
    zq, zkv, zpe = jnp.split(z, [Q_LORA, Q_LORA + KV_LORA], axis=-1)
    q = (rmsnorm(zq, g_q_a) @ w_uq).reshape(bsz, s, MLA_HEADS, QK_HEAD_DIM)
    kv = (rmsnorm(zkv, g_kv_a) @ w_ukv).reshape(bsz, s, MLA_HEADS, QK_NOPE_DIM + V_HEAD_DIM)
    k_nope, v = jnp.split(kv, [QK_NOPE_DIM], axis=-1)
    k_pe = jnp.broadcast_to(zpe[:, :, None, :], (bsz, s, MLA_HEADS, QK_ROPE_DIM))
    k = jnp.concatenate([k_nope, k_pe], axis=-1)
    q = rmsnorm(q, q_norm)
    k = rmsnorm(k, k_norm)
    q = jnp.concatenate([q[..., :QK_NOPE_DIM], rope(q[..., QK_NOPE_DIM:], cos, sin)], axis=-1)
    k = jnp.concatenate([k[..., :QK_NOPE_DIM], rope(k[..., QK_NOPE_DIM:], cos, sin)], axis=-1)
    o = causal_attention(q, k, v)
    return o.reshape(bsz, s, D_MLA)


def hier_moe(h, w_rg, b_rg, w_re, b_re, w_e_gate, w_e_up, w_e_down):
    bsz, s, d = h.shape
    t = h.reshape(-1, d)
    g_logits = (t @ w_rg).astype(jnp.float32) + b_rg.astype(jnp.float32)
    p_group = jax.nn.softmax(g_logits, axis=-1)
    g_idx = jnp.argmax(g_logits, axis=-1)
    p_g = jnp.take_along_axis(p_group, g_idx[:, None], axis=-1)
    e_logits = ((t @ w_re).astype(jnp.float32) + b_re.astype(jnp.float32)).reshape(
        -1, N_GROUPS, EXPERTS_PER_GROUP)
    e_in_group = jnp.take_along_axis(e_logits, g_idx[:, None, None], axis=1)[:, 0]
    p_e = jax.nn.softmax(e_in_group, axis=-1)
    top_w, top_i = lax.top_k(p_e, TOP_K)
    top_w = top_w / jnp.sum(top_w, axis=-1, keepdims=True) * p_g
    expert_id = g_idx[:, None] * EXPERTS_PER_GROUP + top_i
    combine = jnp.sum(jax.nn.one_hot(expert_id, N_EXPERTS, dtype=jnp.float32)
                      * top_w[..., None], axis=1).astype(t.dtype)
    out = jnp.zeros_like(t)
    for e in range(N_EXPERTS):
        hid = jax.nn.silu(t @ w_e_gate[e]) * (t @ w_e_up[e])
        out = out + combine[:, e:e + 1] * (hid @ w_e_down[e])
    return out.reshape(bsz, s, d)


def setup_inputs(seed: int = 0) -> dict:
    key = jax.random.key(seed)
    ks = iter(jax.random.split(key, 40))
    L = DEPTH
    nrm = lambda shape, scale: jax.random.normal(next(ks), shape, jnp.float32) * scale
    gain = lambda shape: 1.0 + nrm(shape, 0.02)
    x = nrm((BATCH, SEQ, D_MODEL), 1.0)
    c = nrm((BATCH, D_MODEL), 1.0)
    offset = jax.random.randint(next(ks), (BATCH, 1), 0, 4096, dtype=jnp.int32)
    positions = (offset + jnp.arange(SEQ, dtype=jnp.int32)[None, :]).astype(jnp.int32)
    return {
        'x': x,
        'c': c,
        'positions': positions,
        'w_ada': nrm((L, D_MODEL, 6 * D_MODEL), D_MODEL ** -0.5),
        'b_ada': nrm((L, 6 * D_MODEL), 0.01),
        'g_norm_mix': gain((L, D_MODEL)),
        'w_in': nrm((L, D_MODEL, D_IN), D_MODEL ** -0.5),
        'mu_shift': jax.random.uniform(next(ks), (L, RWKV_COLS), jnp.float32),
        'w0': -2.0 + nrm((L, D_RWKV), 0.5),
        'w_decay_up': nrm((L, DECAY_LORA, D_RWKV), 0.5 * DECAY_LORA ** -0.5),
        'a0': nrm((L, D_RWKV), 0.1),
        'w_a_up': nrm((L, AAA_LORA, D_RWKV), AAA_LORA ** -0.5),
        'w_g_up': nrm((L, GATE_LORA, D_RWKV), GATE_LORA ** -0.5),
        'k_k': 0.85 + nrm((L, D_RWKV), 0.05),
        'k_a': 1.0 + nrm((L, D_RWKV), 0.05),
        'r_k': nrm((L, RWKV_HEADS, RWKV_HEAD_DIM), 0.1),
        'ln_x_w': gain((L, D_RWKV)),
        'ln_x_b': nrm((L, D_RWKV), 0.01),
        'g_q_a': gain((L, Q_LORA)),
        'w_uq': nrm((L, Q_LORA, MLA_HEADS * QK_HEAD_DIM), Q_LORA ** -0.5),
        'g_kv_a': gain((L, KV_LORA)),
        'w_ukv': nrm((L, KV_LORA, MLA_HEADS * (QK_NOPE_DIM + V_HEAD_DIM)), KV_LORA ** -0.5),
        'q_norm': gain((L, QK_HEAD_DIM)),
        'k_norm': gain((L, QK_HEAD_DIM)),
        'w_branch': nrm((L, N_BRANCHES, D_BRANCH, D_MODEL), D_BRANCH ** -0.5),
        'w_out': nrm((L, D_MODEL, D_MODEL), D_MODEL ** -0.5),
        'g_norm_ffn': gain((L, D_MODEL)),
        'w_router_group': nrm((L, D_MODEL, N_GROUPS), D_MODEL ** -0.5),
        'b_router_group': nrm((L, N_GROUPS), 0.01),
        'w_router_expert': nrm((L, D_MODEL, N_EXPERTS), D_MODEL ** -0.5),
        'b_router_expert': nrm((L, N_EXPERTS), 0.01),
        'w_e_gate': nrm((L, N_EXPERTS, D_MODEL, D_EXPERT), D_MODEL ** -0.5),
        'w_e_up': nrm((L, N_EXPERTS, D_MODEL, D_EXPERT), D_MODEL ** -0.5),
        'w_e_down': nrm((L, N_EXPERTS, D_EXPERT, D_MODEL), D_EXPERT ** -0.5),
    }


def reference(x, c, positions, w_ada, b_ada, g_norm_mix, w_in, mu_shift, w0, w_decay_up,
              a0, w_a_up, w_g_up, k_k, k_a, r_k, ln_x_w, ln_x_b, g_q_a, w_uq, g_kv_a,
              w_ukv, q_norm, k_norm, w_branch, w_out, g_norm_ffn, w_router_group,
              b_router_group, w_router_expert, b_router_expert, w_e_gate, w_e_up, w_e_down):
    freqs = ROPE_THETA ** (-(jnp.arange(0, QK_ROPE_DIM, 2, dtype=jnp.float32) / QK_ROPE_DIM))
    ang = positions.astype(jnp.float32)[..., None] * freqs
    cos = jnp.cos(ang)[:, :, None, :].astype(x.dtype)
    sin = jnp.sin(ang)[:, :, None, :].astype(x.dtype)
    for l in range(DEPTH):
        mod = jax.nn.silu(c) @ w_ada[l] + b_ada[l]
        sh_m, sc_m, gt_m, sh_f, sc_f, gt_f = jnp.split(mod, 6, axis=-1)

        h = modulate(rmsnorm(x, g_norm_mix[l]), sh_m, sc_m)
        z = h @ w_in[l]
        z_rwkv, z_mla, z_gates = jnp.split(z, [RWKV_COLS, RWKV_COLS + MLA_COLS], axis=-1)
        y_rwkv = rwkv7_branch(z_rwkv, mu_shift[l], w0[l], w_decay_up[l], a0[l], w_a_up[l],
                              w_g_up[l], k_k[l], k_a[l], r_k[l], ln_x_w[l], ln_x_b[l])
        y_mla = mla_branch(z_mla, cos, sin, g_q_a[l], w_uq[l], g_kv_a[l], w_ukv[l],
                           q_norm[l], k_norm[l])
        gate_rwkv, gate_mla = jnp.split(jax.nn.sigmoid(z_gates), N_BRANCHES, axis=-1)
        merged = gate_rwkv * (y_rwkv @ w_branch[l, 0]) + gate_mla * (y_mla @ w_branch[l, 1])
        x = x + gt_m[:, None, :] * (merged @ w_out[l])

        h2 = modulate(rmsnorm(x, g_norm_ffn[l]), sh_f, sc_f)
        x = x + gt_f[:, None, :] * hier_moe(h2, w_router_group[l], b_router_group[l],
                                            w_router_expert[l], b_router_expert[l],
                                            w_e_gate[l], w_e_up[l], w_e_down[l])
    return x
```

```python
import functools

import numpy as np
import jax
import jax.numpy as jnp
from jax import lax
from jax.experimental import pallas as pl
from jax.experimental.pallas import tpu as pltpu

F32 = jnp.float32
BF16 = jnp.bfloat16

D_MODEL = 1024
EPS = 1e-6
RWKV_HEADS = 8
RWKV_HEAD_DIM = 64
D_RWKV = 512
DECAY_LORA = 64
AAA_LORA = 64
GATE_LORA = 128
LN_X_EPS = 64e-5
MLA_HEADS = 8
QK_NOPE_DIM = 64
QK_ROPE_DIM = 32
QK_HEAD_DIM = 96
V_HEAD_DIM = 64
D_MLA = 512
Q_LORA = 256
KV_LORA = 128
ROPE_THETA = 10000.0
RWKV_COLS = 1792
MLA_COLS = 416
MLA_COLS_PAD = 512
N_GROUPS = 4
EXPERTS_PER_GROUP = 8
N_EXPERTS = 32
D_EXPERT = 256

LANES = 128
CHUNK = 64
HEAD_PAD = 128
NEG_BIG = -1e30
VMEM_LIMIT = 56 * 1024 * 1024


def _cparams(*sem):
    return pltpu.CompilerParams(dimension_semantics=sem, vmem_limit_bytes=VMEM_LIMIT)


def _dot(a, b):
    return jnp.dot(a, b, preferred_element_type=F32)


def _dot_nt(a, b):
    return lax.dot_general(a, b, (((1,), (1,)), ((), ())), preferred_element_type=F32)


def _dot_tn(a, b):
    return lax.dot_general(a, b, (((0,), (0,)), ((), ())), preferred_element_type=F32)


def _mm(a, b):
    return _dot(a.astype(BF16), b.astype(BF16))


def _split(a):
    hi = a.astype(BF16)
    lo = (a - hi.astype(F32)).astype(BF16)
    return hi, lo


def _mm_rhs_exact(a, b):
    hi, lo = _split(a)
    return _dot(hi, b) + _dot(lo, b)


def _mm_lhs_exact(a, b):
    hi, lo = _split(b)
    return _dot(a, hi) + _dot(a, lo)


def _mm3(a, b):
    ah, al = _split(a)
    bh, bl = _split(b)
    return _dot(ah, bh) + (_dot(ah, bl) + _dot(al, bh))


def _sigmoid(x):
    return 1.0 / (1.0 + jnp.exp(-x))


def _const_spec(shape):
    nd = len(shape)
    return pl.BlockSpec(shape, lambda *_: (0,) * nd)


def _ada_kernel(c_ref, w_ref, b_ref, o_ref):
    c = c_ref[...]
    s = c * _sigmoid(c)
    o_ref[...] = _mm3(s, w_ref[...]) + b_ref[...]


def _ada(c, w, b):
    bsz, d = c.shape
    n = w.shape[1]
    tn = 1024
    return pl.pallas_call(
        _ada_kernel,
        grid=(n // tn,),
        in_specs=[pl.BlockSpec((bsz, d), lambda j: (0, 0)),
                  pl.BlockSpec((d, tn), lambda j: (0, j)),
                  pl.BlockSpec((1, tn), lambda j: (0, j))],
        out_specs=pl.BlockSpec((bsz, tn), lambda j: (0, j)),
        out_shape=jax.ShapeDtypeStruct((bsz, n), F32),
        compiler_params=_cparams("arbitrary"),
        name="ada",
    )(c, w, b.reshape(1, n))


def _inproj_kernel(x_ref, sh_ref, sc_ref, g_ref, wr_ref, wm_ref, wg_ref,
                   zr_ref, zm_ref, sg_ref):
    x = x_ref[...]
    ms = jnp.mean(x * x, axis=-1, keepdims=True)
    h = x * lax.rsqrt(ms + EPS) * g_ref[...]
    h = h * (1.0 + sc_ref[0]) + sh_ref[0]
    hb = h.astype(BF16)
    zr_ref[...] = _dot(hb, wr_ref[...])
    zm_ref[...] = _dot(hb, wm_ref[...])
    sg_ref[...] = _sigmoid(_dot(hb, wg_ref[...])).astype(BF16)


def _inproj(x2, sh, sc, g, wr, wm, wg, seq):
    n, d = x2.shape
    tm = 512
    per_b = seq // tm
    bmap = lambda i: (i // per_b, 0, 0)
    return pl.pallas_call(
        _inproj_kernel,
        grid=(n // tm,),
        in_specs=[pl.BlockSpec((tm, d), lambda i: (i, 0)),
                  pl.BlockSpec((1, 1, d), bmap),
                  pl.BlockSpec((1, 1, d), bmap),
                  _const_spec((1, d)),
                  _const_spec(wr.shape), _const_spec(wm.shape), _const_spec(wg.shape)],
        out_specs=[pl.BlockSpec((tm, wr.shape[1]), lambda i: (i, 0)),
                   pl.BlockSpec((tm, wm.shape[1]), lambda i: (i, 0)),
                   pl.BlockSpec((tm, wg.shape[1]), lambda i: (i, 0))],
        out_shape=[jax.ShapeDtypeStruct((n, wr.shape[1]), F32),
                   jax.ShapeDtypeStruct((n, wm.shape[1]), F32),
                   jax.ShapeDtypeStruct((n, wg.shape[1]), BF16)],
        compiler_params=_cparams("arbitrary"),
        name="inproj",
    )(x2, sh, sc, g, wr, wm, wg)


def _rwkv_prep_kernel(z_ref, prev_ref, mu_ref, w0_ref, wdec_ref, a0_ref, wa_ref, wgu_ref,
                      kk_ref, ka_ref, rk_ref, ones_ref, tri_ref, blk_ref,
                      rp_ref, am_ref, bm_ref, km_ref, bh_ref, kh_ref, v_ref,
                      pc_ref, bv_ref, g_ref):
    tt = z_ref.shape[1]
    z = z_ref[0]
    prev = prev_ref[0][7:8, :]
    prev = jnp.where(pl.program_id(1) == 0, 0.0, prev)
    row = lax.broadcasted_iota(jnp.int32, (tt, 1), 0)
    zs = jnp.where(row == 0, prev, pltpu.roll(z, 1, axis=0))
    z = z + (zs - z) * mu_ref[...]
    zr = z[:, 0:D_RWKV]
    zk = z[:, D_RWKV:2 * D_RWKV]
    zv = z[:, 2 * D_RWKV:3 * D_RWKV]
    zwa = z[:, 3 * D_RWKV:3 * D_RWKV + DECAY_LORA + AAA_LORA]
    zg = z[:, 3 * D_RWKV + DECAY_LORA + AAA_LORA:]

    u = -(w0_ref[...] + _mm3(jnp.tanh(zwa), wdec_ref[...]))
    softplus = jnp.maximum(u, 0.0) + jnp.log(1.0 + jnp.exp(-jnp.abs(u)))
    logw = -jnp.exp(-softplus - 0.5)
    a = _sigmoid(a0_ref[...] + _mm(zwa, wa_ref[...]))
    g = _mm(_sigmoid(zg), wgu_ref[...])

    ones_blk = ones_ref[...]
    xk = zk * kk_ref[...]
    ss = _mm_rhs_exact(xk * xk, ones_blk)
    kk = xk * lax.rsqrt(jnp.maximum(ss, 1e-24))
    k = zk * (1.0 + (a - 1.0) * ka_ref[...])
    b = kk * a
    bonus = _mm_rhs_exact(zr * k * rk_ref[...], ones_blk)

    cum = _mm_lhs_exact(tri_ref[...], logw)
    tot = _mm_lhs_exact(blk_ref[...], logw)
    e_neg = jnp.exp(-cum)
    e_rem = jnp.exp(tot - cum)
    rp_ref[0] = (zr * jnp.exp(cum)).astype(BF16)
    am_ref[0] = (-kk * jnp.exp(cum - logw)).astype(BF16)
    bm_ref[0] = (b * e_neg).astype(BF16)
    km_ref[0] = (k * e_neg).astype(BF16)
    bh_ref[0] = (b * e_rem).astype(BF16)
    kh_ref[0] = (k * e_rem).astype(BF16)
    v_ref[0] = zv.astype(BF16)
    bv_ref[0] = bonus * zv
    g_ref[0] = g
    nc = tt // CHUNK
    pick = (lax.broadcasted_iota(jnp.int32, (nc, tt), 1)
            == CHUNK * lax.broadcasted_iota(jnp.int32, (nc, tt), 0))
    pc_ref[0] = _mm_lhs_exact(jnp.where(pick, 1.0, 0.0).astype(BF16), jnp.exp(tot))


def _rwkv_prep(zr3, mu, w0, wdec, a0, wa, wgu, k_k, k_a, r_k):
    bsz, seq, cols = zr3.shape
    tt = 512
    d = D_RWKV
    ones_blk = jnp.asarray(np.kron(np.eye(RWKV_HEADS), np.ones((RWKV_HEAD_DIM, RWKV_HEAD_DIM))), BF16)
    cidx = np.arange(tt) // CHUNK
    same = cidx[:, None] == cidx[None, :]
    tri = jnp.asarray(same & (np.arange(tt)[None, :] <= np.arange(tt)[:, None]), BF16)
    blk = jnp.asarray(same, BF16)
    tok = lambda b, j: (b, j, 0)
    big = pl.BlockSpec((1, tt, d), tok)
    out_bf = jax.ShapeDtypeStruct((bsz, seq, d), BF16)
    out_f = jax.ShapeDtypeStruct((bsz, seq, d), F32)
    return pl.pallas_call(
        _rwkv_prep_kernel,
        grid=(bsz, seq // tt),
        in_specs=[pl.BlockSpec((1, tt, cols), tok),
                  pl.BlockSpec((1, 8, cols), lambda b, j: (b, jnp.maximum(j * (tt // 8) - 1, 0), 0)),
                  _const_spec((1, cols)), _const_spec((1, d)), _const_spec(wdec.shape),
                  _const_spec((1, d)), _const_spec(wa.shape), _const_spec(wgu.shape),
                  _const_spec((1, d)), _const_spec((1, d)), _const_spec((1, d)),
                  _const_spec((d, d)), _const_spec((tt, tt)), _const_spec((tt, tt))],
        out_specs=[big] * 7 + [pl.BlockSpec((1, tt // CHUNK, d), tok), big, big],
        out_shape=[out_bf] * 7 + [jax.ShapeDtypeStruct((bsz, seq // CHUNK, d), F32), out_f, out_f],
        compiler_params=_cparams("arbitrary", "arbitrary"),
        name="rwkv_prep",
    )(zr3, zr3, mu, w0, wdec, a0, wa, wgu, k_k, k_a, r_k, ones_blk, tri, blk)


def _rwkv_scan_kernel(rp_ref, am_ref, bm_ref, km_ref, bh_ref, kh_ref, v_ref, pc_ref,
                      bv_ref, g_ref, lnw_ref, lnb_ref, o_ref, h_scr, y_scr):
    tb = rp_ref.shape[1]
    nc = tb // CHUNK
    hp = pl.program_id(2)

    @pl.when(pl.program_id(1) == 0)
    def _():
        h_scr[hp] = jnp.zeros((LANES, LANES), F32)

    lane = lax.broadcasted_iota(jnp.int32, (1, LANES), 1)
    head_masks = [lane < RWKV_HEAD_DIM, lane >= RWKV_HEAD_DIM]
    r64 = lax.broadcasted_iota(jnp.int32, (CHUNK, CHUNK), 0)
    c64 = lax.broadcasted_iota(jnp.int32, (CHUNK, CHUNK), 1)
    strict = c64 < r64
    incl = c64 <= r64
    eye64 = c64 == r64
    level_masks = [((r64 >> (k + 1)) == (c64 >> (k + 1))) & ((r64 >> k) == (c64 >> k) + 1)
                   for k in range(6)]
    r128 = lax.broadcasted_iota(jnp.int32, (LANES, LANES), 0)
    c128 = lax.broadcasted_iota(jnp.int32, (LANES, LANES), 1)
    blockdiag = (r128 < RWKV_HEAD_DIM) == (c128 < RWKV_HEAD_DIM)
    eye = r128 == c128
    zero_bf = jnp.zeros((), BF16)

    gys, mhs = [], []
    for c in range(nc):
        sl = pl.ds(c * CHUNK, CHUNK)
        rp = rp_ref[0, sl, :]
        am = am_ref[0, sl, :]
        bm = bm_ref[0, sl, :]
        km = km_ref[0, sl, :]
        bh = bh_ref[0, sl, :]
        kh = kh_ref[0, sl, :]
        v = v_ref[0, sl, :]
        z_sum = None
        gy = jnp.concatenate([rp.astype(F32), jnp.zeros((CHUNK, LANES), F32)], axis=1)
        for hm in head_masks:
            am_h = jnp.where(hm, am, zero_bf)
            rp_h = jnp.where(hm, rp, zero_bf)
            v_h = jnp.where(hm, v, zero_bf)
            a_ab = jnp.where(strict, _dot_nt(am_h, bm), 0.0)
            a_ak = jnp.where(strict, _dot_nt(am_h, km), 0.0)
            a_rb = jnp.where(incl, _dot_nt(rp_h, bm), 0.0)
            a_rk = jnp.where(incl, _dot_nt(rp_h, km), 0.0)
            tinv = jnp.where(eye64, 1.0, 0.0) + jnp.where(level_masks[0], a_ab, 0.0)
            for lm in level_masks[1:]:
                tb = tinv.astype(BF16)
                inner = _dot(jnp.where(lm, a_ab, 0.0).astype(BF16), tb)
                tinv = tinv + _dot(tb, inner.astype(BF16))
            z0 = jnp.concatenate([am_h, _mm(a_ak, v_h).astype(BF16)], axis=1)
            zc = _dot(tinv.astype(BF16), z0)
            gy = gy + _mm(a_rb, zc)
            gy = gy + jnp.concatenate([jnp.zeros((CHUNK, LANES), F32), _mm(a_rk, v_h)], axis=1)
            z_sum = zc if z_sum is None else z_sum + zc
        pc = pc_ref[0, c:c + 1, :]
        mh = _dot_tn(bh, z_sum.astype(BF16))
        m = jnp.where(blockdiag, mh[:, :LANES], 0.0) + jnp.where(eye, pc, 0.0)
        hadd = jnp.where(blockdiag, mh[:, LANES:] + _dot_tn(kh, v), 0.0)
        gys.append(gy)
        mhs.append((m, hadd))

    h = h_scr[hp]
    for c in range(nc):
        hb = h.astype(BF16)
        gy = gys[c]
        m, hadd = mhs[c]
        y_scr[pl.ds(c * CHUNK, CHUNK), :] = _dot(gy[:, :LANES].astype(BF16), hb) + gy[:, LANES:]
        h = _dot(m.astype(BF16), hb) + hadd
    h_scr[hp] = h

    y = y_scr[...]
    avg = jnp.where(blockdiag, 1.0 / RWKV_HEAD_DIM, 0.0).astype(BF16)
    mean = _mm_rhs_exact(y, avg)
    yc = y - mean
    var = _mm_rhs_exact(yc * yc, avg)
    yn = yc * lax.rsqrt(var + LN_X_EPS)
    out = (yn * lnw_ref[...] + lnb_ref[...] + bv_ref[0]) * g_ref[0]
    o_ref[0] = out.astype(BF16)


def _rwkv_scan(rp, am, bm, km, bh, kh, v, pc, bv, g, lnw, lnb):
    bsz, seq, d = rp.shape
    tb = 512
    npairs = d // LANES
    tok = pl.BlockSpec((1, tb, LANES), lambda b, t, p: (b, t, p))
    vec = pl.BlockSpec((1, LANES), lambda b, t, p: (0, p))
    return pl.pallas_call(
        _rwkv_scan_kernel,
        grid=(bsz, seq // tb, npairs),
        in_specs=[tok] * 7 + [pl.BlockSpec((1, tb // CHUNK, LANES), lambda b, t, p: (b, t, p)),
                              tok, tok, vec, vec],
        out_specs=tok,
        out_shape=jax.ShapeDtypeStruct((bsz, seq, d), BF16),
        scratch_shapes=[pltpu.VMEM((npairs, LANES, LANES), F32),
                        pltpu.VMEM((tb, LANES), F32)],
        compiler_params=_cparams("arbitrary", "arbitrary", "arbitrary"),
        name="rwkv_scan",
    )(rp, am, bm, km, bh, kh, v, pc, bv, g, lnw, lnb)


def _rope(x, cos, sin, lane):
    lo = QK_NOPE_DIM + QK_ROPE_DIM // 2
    partner = jnp.where(lane < lo,
                        -pltpu.roll(x, LANES - QK_ROPE_DIM // 2, axis=1),
                        pltpu.roll(x, QK_ROPE_DIM // 2, axis=1))
    return x * cos + partner * sin


def _mla_prep_kernel(z_ref, pos_ref, freq_ref, gq_ref, wuq_ref, gkv_ref, wukv_ref,
                     qn_ref, kn_ref, q_ref, k_ref, v_ref):
    z = z_ref[0]
    zq = z[:, :Q_LORA]
    zkv = z[:, Q_LORA:Q_LORA + KV_LORA]
    zpe = z[:, Q_LORA + KV_LORA:]
    ang = pos_ref[0] * freq_ref[...]
    cos = jnp.cos(ang)
    sin = jnp.sin(ang)
    lane = lax.broadcasted_iota(jnp.int32, (1, LANES), 1)
    nope = lane < QK_NOPE_DIM

    qa = zq * lax.rsqrt(jnp.mean(zq * zq, axis=-1, keepdims=True) + EPS) * gq_ref[...]
    q_all = _mm(qa, wuq_ref[...])
    kva = zkv * lax.rsqrt(jnp.mean(zkv * zkv, axis=-1, keepdims=True) + EPS) * gkv_ref[...]
    kv_all = _mm(kva, wukv_ref[...])
    pe = pltpu.roll(zpe, QK_NOPE_DIM, axis=1)
    scale = QK_HEAD_DIM ** -0.5
    for h in range(MLA_HEADS):
        q = q_all[:, h * HEAD_PAD:(h + 1) * HEAD_PAD]
        ms = jnp.sum(q * q, axis=-1, keepdims=True) * (1.0 / QK_HEAD_DIM)
        q = q * lax.rsqrt(ms + EPS) * qn_ref[...]
        q = _rope(q, cos, sin, lane) * scale
        q_ref[0, h] = q.astype(BF16)
        kv = kv_all[:, h * HEAD_PAD:(h + 1) * HEAD_PAD]
        k = jnp.where(nope, kv, pe)
        ms = jnp.sum(k * k, axis=-1, keepdims=True) * (1.0 / QK_HEAD_DIM)
        k = k * lax.rsqrt(ms + EPS) * kn_ref[...]
        k_ref[0, h] = _rope(k, cos, sin, lane).astype(BF16)
        v_ref[0, h] = pltpu.roll(kv, LANES - QK_NOPE_DIM, axis=1)[:, :V_HEAD_DIM].astype(BF16)


def _mla_prep(zm3, pos, freq, gq, wuq, gkv, wukv, qn, kn):
    bsz, seq, cols = zm3.shape
    tm = 512
    tok = lambda b, j: (b, j, 0)
    head = lambda b, j: (b, 0, j, 0)
    return pl.pallas_call(
        _mla_prep_kernel,
        grid=(bsz, seq // tm),
        in_specs=[pl.BlockSpec((1, tm, cols), tok), pl.BlockSpec((1, tm, 1), tok),
                  _const_spec((1, LANES)), _const_spec((1, Q_LORA)), _const_spec(wuq.shape),
                  _const_spec((1, KV_LORA)), _const_spec(wukv.shape),
                  _const_spec((1, LANES)), _const_spec((1, LANES))],
        out_specs=[pl.BlockSpec((1, MLA_HEADS, tm, HEAD_PAD), head),
                   pl.BlockSpec((1, MLA_HEADS, tm, HEAD_PAD), head),
                   pl.BlockSpec((1, MLA_HEADS, tm, V_HEAD_DIM), head)],
        out_shape=[jax.ShapeDtypeStruct((bsz, MLA_HEADS, seq, HEAD_PAD), BF16),
                   jax.ShapeDtypeStruct((bsz, MLA_HEADS, seq, HEAD_PAD), BF16),
                   jax.ShapeDtypeStruct((bsz, MLA_HEADS, seq, V_HEAD_DIM), BF16)],
        compiler_params=_cparams("arbitrary", "arbitrary"),
        name="mla_prep",
    )(zm3, pos, freq, gq, wuq, gkv, wukv, qn, kn)


def _attn_kernel(qi_ref, kj_ref, q_ref, k_ref, v_ref, o_ref, m_scr, l_scr, acc_scr):
    p = pl.program_id(1)
    qi = qi_ref[p]
    kj = kj_ref[p]
    bq = q_ref.shape[2]
    bk = k_ref.shape[2]

    @pl.when(kj == 0)
    def _():
        m_scr[...] = jnp.full(m_scr.shape, NEG_BIG, F32)
        l_scr[...] = jnp.zeros(l_scr.shape, F32)
        acc_scr[...] = jnp.zeros(acc_scr.shape, F32)

    row = qi * bq + lax.broadcasted_iota(jnp.int32, (bq, bk), 0)
    col = kj * bk + lax.broadcasted_iota(jnp.int32, (bq, bk), 1)
    visible = col <= row
    for h in range(MLA_HEADS):
        s = _dot_nt(q_ref[0, h], k_ref[0, h])
        s = jnp.where(visible, s, NEG_BIG)
        m_prev = m_scr[h]
        m_new = jnp.maximum(m_prev, jnp.max(s, axis=-1, keepdims=True))
        alpha = jnp.exp(m_prev - m_new)
        pr = jnp.exp(s - m_new)
        l_scr[h] = alpha * l_scr[h] + jnp.sum(pr, axis=-1, keepdims=True)
        acc_scr[h] = alpha * acc_scr[h] + _dot(pr.astype(BF16), v_ref[0, h])
        m_scr[h] = m_new

    @pl.when(kj == qi)
    def _():
        for h in range(MLA_HEADS):
            o = acc_scr[h] / l_scr[h]
            o_ref[0, :, h * V_HEAD_DIM:(h + 1) * V_HEAD_DIM] = o.astype(BF16)


def _attn(q, k, v):
    bsz, nh, seq, _ = q.shape
    bq = bk = 512
    nq = seq // bq
    qi = np.concatenate([np.full(i + 1, i) for i in range(nq)]).astype(np.int32)
    kj = np.concatenate([np.arange(i + 1) for i in range(nq)]).astype(np.int32)
    grid_spec = pltpu.PrefetchScalarGridSpec(
        num_scalar_prefetch=2,
        grid=(bsz, len(qi)),
        in_specs=[pl.BlockSpec((1, nh, bq, HEAD_PAD), lambda b, p, qi, kj: (b, 0, qi[p], 0)),
                  pl.BlockSpec((1, nh, bk, HEAD_PAD), lambda b, p, qi, kj: (b, 0, kj[p], 0)),
                  pl.BlockSpec((1, nh, bk, V_HEAD_DIM), lambda b, p, qi, kj: (b, 0, kj[p], 0))],
        out_specs=pl.BlockSpec((1, bq, nh * V_HEAD_DIM), lambda b, p, qi, kj: (b, qi[p], 0)),
        scratch_shapes=[pltpu.VMEM((nh, bq, 1), F32), pltpu.VMEM((nh, bq, 1), F32),
                        pltpu.VMEM((nh, bq, V_HEAD_DIM), F32)],
    )
    return pl.pallas_call(
        _attn_kernel,
        grid_spec=grid_spec,
        out_shape=jax.ShapeDtypeStruct((bsz, seq, nh * V_HEAD_DIM), BF16),
        compiler_params=_cparams("arbitrary", "arbitrary"),
        name="mla_attn",
    )(jnp.asarray(qi), jnp.asarray(kj), q, k, v)


def _first_index_of_max(vals, lane):
    mx = jnp.max(vals, axis=-1, keepdims=True)
    idx = jnp.min(jnp.where(vals == mx, lane, LANES), axis=-1, keepdims=True)
    return mx, idx


def _post_kernel(yr_ref, ym_ref, sg_ref, x_ref, gt_ref, sh_ref, sc_ref, g_ref,
                 wb0_ref, wb1_ref, wo_ref, wr_ref, br_ref, x1_ref, h2_ref, cmb_ref):
    sg = sg_ref[...].astype(F32)
    merged = (sg[:, :D_MODEL] * _dot(yr_ref[...], wb0_ref[...])
              + sg[:, D_MODEL:] * _dot(ym_ref[...], wb1_ref[...]))
    x1 = x_ref[...] + gt_ref[0] * _mm(merged, wo_ref[...])
    x1_ref[...] = x1
    ms = jnp.mean(x1 * x1, axis=-1, keepdims=True)
    h2 = x1 * lax.rsqrt(ms + EPS) * g_ref[...]
    h2 = h2 * (1.0 + sc_ref[0]) + sh_ref[0]
    h2_ref[...] = h2.astype(BF16)

    logits = _mm3(h2, wr_ref[...]) + br_ref[...]
    lane = lax.broadcasted_iota(jnp.int32, logits.shape, 1).astype(F32)
    gl = jnp.where((lane >= N_EXPERTS) & (lane < N_EXPERTS + N_GROUPS), logits, NEG_BIG)
    gmax, gidx = _first_index_of_max(gl, lane)
    p_g = 1.0 / jnp.sum(jnp.exp(gl - gmax), axis=-1, keepdims=True)
    first = (gidx - N_EXPERTS) * EXPERTS_PER_GROUP
    el = jnp.where((lane >= first) & (lane < first + EXPERTS_PER_GROUP), logits, NEG_BIG)
    m1, i1 = _first_index_of_max(el, lane)
    z = jnp.sum(jnp.exp(el - m1), axis=-1, keepdims=True)
    el2 = jnp.where(lane == i1, NEG_BIG, el)
    m2, i2 = _first_index_of_max(el2, lane)
    p1 = 1.0 / z
    p2 = jnp.exp(m2 - m1) / z
    tot = p1 + p2
    cmb_ref[...] = (jnp.where(lane == i1, p1 / tot * p_g, 0.0)
                    + jnp.where(lane == i2, p2 / tot * p_g, 0.0))


def _post(yr, ym, sg, x2, gt, sh, sc, g, wb0, wb1, wo, wr, br, seq):
    n, d = x2.shape
    tm = 512
    per_b = seq // tm
    bmap = lambda i: (i // per_b, 0, 0)
    rowmap = lambda i: (i, 0)
    return pl.pallas_call(
        _post_kernel,
        grid=(n // tm,),
        in_specs=[pl.BlockSpec((tm, D_RWKV), rowmap), pl.BlockSpec((tm, D_MLA), rowmap),
                  pl.BlockSpec((tm, 2 * d), rowmap), pl.BlockSpec((tm, d), rowmap),
                  pl.BlockSpec((1, 1, d), bmap), pl.BlockSpec((1, 1, d), bmap),
                  pl.BlockSpec((1, 1, d), bmap), _const_spec((1, d)),
                  _const_spec(wb0.shape), _const_spec(wb1.shape), _const_spec(wo.shape),
                  _const_spec(wr.shape), _const_spec(br.shape)],
        out_specs=[pl.BlockSpec((tm, d), rowmap), pl.BlockSpec((tm, d), rowmap),
                   pl.BlockSpec((tm, LANES), rowmap)],
        out_shape=[jax.ShapeDtypeStruct((n, d), F32), jax.ShapeDtypeStruct((n, d), BF16),
                   jax.ShapeDtypeStruct((n, LANES), F32)],
        compiler_params=_cparams("arbitrary"),
        name="post",
    )(yr, ym, sg, x2, gt, sh, sc, g, wb0, wb1, wo, wr, br)


def _moe_kernel(h_ref, cmb_ref, x1_ref, gt_ref, wg_ref, wu_ref, wd_ref, o_ref, acc_scr):
    e = pl.program_id(1)

    @pl.when(e == 0)
    def _():
        acc_scr[...] = jnp.zeros(acc_scr.shape, F32)

    h = h_ref[...]
    gate = _dot(h, wg_ref[0])
    hid = gate * _sigmoid(gate) * _dot(h, wu_ref[0])
    cmb = cmb_ref[...]
    lane = lax.broadcasted_iota(jnp.int32, cmb.shape, 1)
    cw = jnp.sum(jnp.where(lane == e, cmb, 0.0), axis=-1, keepdims=True)
    acc_scr[...] += _mm(hid * cw, wd_ref[0])

    @pl.when(e == pl.num_programs(1) - 1)
    def _():
        o_ref[...] = x1_ref[...] + gt_ref[0] * acc_scr[...]


def _moe(h2, cmb, x1, gt, wg, wu, wd, seq):
    n, d = x1.shape
    ne = wg.shape[0]
    tm = 1024
    per_b = seq // tm
    rowmap = lambda i, e: (i, 0)
    return pl.pallas_call(
        _moe_kernel,
        grid=(n // tm, ne),
        in_specs=[pl.BlockSpec((tm, d), rowmap), pl.BlockSpec((tm, LANES), rowmap),
                  pl.BlockSpec((tm, d), rowmap),
                  pl.BlockSpec((1, 1, d), lambda i, e: (i // per_b, 0, 0)),
                  pl.BlockSpec((1, d, D_EXPERT), lambda i, e: (e, 0, 0)),
                  pl.BlockSpec((1, d, D_EXPERT), lambda i, e: (e, 0, 0)),
                  pl.BlockSpec((1, D_EXPERT, d), lambda i, e: (e, 0, 0))],
        out_specs=pl.BlockSpec((tm, d), rowmap),
        out_shape=jax.ShapeDtypeStruct((n, d), F32),
        scratch_shapes=[pltpu.VMEM((tm, d), F32)],
        compiler_params=_cparams("arbitrary", "arbitrary"),
        name="moe",
    )(h2, cmb, x1, gt, wg, wu, wd)


def _pad_cols(w, n):
    return jnp.pad(w, ((0, 0), (0, n - w.shape[1])))


def kernel(x, c, positions, w_ada, b_ada, g_norm_mix, w_in, mu_shift, w0, w_decay_up, a0, w_a_up, w_g_up, k_k, k_a, r_k, ln_x_w, ln_x_b, g_q_a, w_uq, g_kv_a, w_ukv, q_norm, k_norm, w_branch, w_out, g_norm_ffn, w_router_group, b_router_group, w_router_expert, b_router_expert, w_e_gate, w_e_up, w_e_down):
    bsz, seq, d = x.shape
    n = bsz * seq
    row = lambda a: a.reshape(1, -1)

    freqs = ROPE_THETA ** (-(jnp.arange(0, QK_ROPE_DIM, 2, dtype=F32) / QK_ROPE_DIM))
    freq = jnp.zeros((1, LANES), F32).at[0, QK_NOPE_DIM:QK_NOPE_DIM + QK_ROPE_DIM].set(
        jnp.concatenate([freqs, freqs]))
    pos = positions.astype(F32).reshape(bsz, seq, 1)

    x2 = x.reshape(n, d)
    for l in range(w_ada.shape[0]):
        mod = _ada(c, w_ada[l], b_ada[l])
        sh_m, sc_m, gt_m, sh_f, sc_f, gt_f = [m.reshape(bsz, 1, d) for m in jnp.split(mod, 6, axis=-1)]

        w_in_l = w_in[l].astype(BF16)
        wr = w_in_l[:, :RWKV_COLS]
        wm = _pad_cols(w_in_l[:, RWKV_COLS:RWKV_COLS + MLA_COLS], MLA_COLS_PAD)
        wg = w_in_l[:, RWKV_COLS + MLA_COLS:]
        zr, zm, sg = _inproj(x2, sh_m, sc_m, row(g_norm_mix[l]), wr, wm, wg, seq)

        zeros_lora = jnp.zeros((DECAY_LORA, D_RWKV), F32)
        wdec = jnp.concatenate([w_decay_up[l], zeros_lora], axis=0)
        wa = jnp.concatenate([zeros_lora, w_a_up[l]], axis=0)
        prep = _rwkv_prep(zr.reshape(bsz, seq, RWKV_COLS), row(mu_shift[l]), row(w0[l]), wdec,
                          row(a0[l]), wa, w_g_up[l], row(k_k[l]), row(k_a[l]), row(r_k[l]))
        y_rwkv = _rwkv_scan(*prep, row(ln_x_w[l]), row(ln_x_b[l]))

        wuq = w_uq[l].reshape(Q_LORA, MLA_HEADS, QK_HEAD_DIM)
        wuq = jnp.pad(wuq, ((0, 0), (0, 0), (0, HEAD_PAD - QK_HEAD_DIM))).reshape(Q_LORA, -1)
        qn = _pad_cols(row(q_norm[l]), LANES)
        kn = _pad_cols(row(k_norm[l]), LANES)
        q, k, v = _mla_prep(zm.reshape(bsz, seq, MLA_COLS_PAD), pos, freq, row(g_q_a[l]),
                            wuq.astype(BF16), row(g_kv_a[l]), w_ukv[l].astype(BF16), qn, kn)
        y_mla = _attn(q, k, v)

        w_router = _pad_cols(jnp.concatenate([w_router_expert[l], w_router_group[l]], axis=1), LANES)
        b_router = _pad_cols(row(jnp.concatenate([b_router_expert[l], b_router_group[l]])), LANES)
        x1, h2, cmb = _post(y_rwkv.reshape(n, D_RWKV), y_mla.reshape(n, D_MLA), sg, x2,
                            gt_m, sh_f, sc_f, row(g_norm_ffn[l]),
                            w_branch[l, 0].astype(BF16), w_branch[l, 1].astype(BF16),
                            w_out[l].astype(BF16), w_router, b_router, seq)

        x2 = _moe(h2, cmb, x1, gt_f, w_e_gate[l].astype(BF16), w_e_up[l].astype(BF16),
                  w_e_down[l].astype(BF16), seq)
    return x2.reshape(bsz, seq, d)
```

```python
import functools

import numpy as np
import jax
import jax.numpy as jnp
from jax import lax
from jax.experimental import pallas as pl
from jax.experimental.pallas import tpu as pltpu

F32 = jnp.float32
BF16 = jnp.bfloat16

D_MODEL = 1024
EPS = 1e-6
RWKV_HEADS = 8
RWKV_HEAD_DIM = 64
D_RWKV = 512
DECAY_LORA = 64
AAA_LORA = 64
GATE_LORA = 128
LN_X_EPS = 64e-5
MLA_HEADS = 8
QK_NOPE_DIM = 64
QK_ROPE_DIM = 32
QK_HEAD_DIM = 96
V_HEAD_DIM = 64
D_MLA = 512
Q_LORA = 256
KV_LORA = 128
ROPE_THETA = 10000.0
RWKV_COLS = 1792
MLA_COLS = 416
MLA_COLS_PAD = 512
N_GROUPS = 4
EXPERTS_PER_GROUP = 8
N_EXPERTS = 32
D_EXPERT = 256

LANES = 128
CHUNK = 64
HEAD_PAD = 128
NEG_BIG = -1e30
LOG2_E = 1.4426950408889634
VMEM_LIMIT = 56 * 1024 * 1024


def _cparams(*sem):
    return pltpu.CompilerParams(dimension_semantics=sem, vmem_limit_bytes=VMEM_LIMIT)


def _dot(a, b):
    return jnp.dot(a, b, preferred_element_type=F32)


def _dot_nt(a, b):
    return lax.dot_general(a, b, (((1,), (1,)), ((), ())), preferred_element_type=F32)


def _dot_tn(a, b):
    return lax.dot_general(a, b, (((0,), (0,)), ((), ())), preferred_element_type=F32)


def _mm(a, b):
    return _dot(a.astype(BF16), b.astype(BF16))


def _split(a):
    hi = a.astype(BF16)
    lo = (a - hi.astype(F32)).astype(BF16)
    return hi, lo


def _mm_rhs_exact(a, b):
    hi, lo = _split(a)
    return _dot(hi, b) + _dot(lo, b)


def _mm_lhs_exact(a, b):
    hi, lo = _split(b)
    return _dot(a, hi) + _dot(a, lo)


def _mm3(a, b):
    ah, al = _split(a)
    bh, bl = _split(b)
    return _dot(ah, bh) + (_dot(ah, bl) + _dot(al, bh))


def _sigmoid(x):
    return 1.0 / (1.0 + jnp.exp(-x))


def _const_spec(shape):
    nd = len(shape)
    return pl.BlockSpec(shape, lambda *_: (0,) * nd)


def _ada_kernel(c_ref, w_ref, b_ref, o_ref):
    c = c_ref[...]
    s = c * _sigmoid(c)
    o_ref[...] = _mm3(s, w_ref[...]) + b_ref[...]


def _ada(c, w, b):
    bsz, d = c.shape
    n = w.shape[1]
    tn = 1024
    return pl.pallas_call(
        _ada_kernel,
        grid=(n // tn,),
        in_specs=[pl.BlockSpec((bsz, d), lambda j: (0, 0)),
                  pl.BlockSpec((d, tn), lambda j: (0, j)),
                  pl.BlockSpec((1, tn), lambda j: (0, j))],
        out_specs=pl.BlockSpec((bsz, tn), lambda j: (0, j)),
        out_shape=jax.ShapeDtypeStruct((bsz, n), F32),
        compiler_params=_cparams("arbitrary"),
        name="ada",
    )(c, w, b.reshape(1, n))


def _inproj_kernel(x_ref, sh_ref, sc_ref, g_ref, wr_ref, wm_ref, wg_ref,
                   zr_ref, zm_ref, sg_ref):
    x = x_ref[...]
    ms = jnp.mean(x * x, axis=-1, keepdims=True)
    h = x * lax.rsqrt(ms + EPS) * g_ref[...]
    h = h * (1.0 + sc_ref[0]) + sh_ref[0]
    hb = h.astype(BF16)
    zr_ref[...] = _dot(hb, wr_ref[...])
    zm_ref[...] = _dot(hb, wm_ref[...])
    sg_ref[...] = _sigmoid(_dot(hb, wg_ref[...])).astype(BF16)


def _inproj(x2, sh, sc, g, wr, wm, wg, seq):
    n, d = x2.shape
    tm = 512
    per_b = seq // tm
    bmap = lambda i: (i // per_b, 0, 0)
    return pl.pallas_call(
        _inproj_kernel,
        grid=(n // tm,),
        in_specs=[pl.BlockSpec((tm, d), lambda i: (i, 0)),
                  pl.BlockSpec((1, 1, d), bmap),
                  pl.BlockSpec((1, 1, d), bmap),
                  _const_spec((1, d)),
                  _const_spec(wr.shape), _const_spec(wm.shape), _const_spec(wg.shape)],
        out_specs=[pl.BlockSpec((tm, wr.shape[1]), lambda i: (i, 0)),
                   pl.BlockSpec((tm, wm.shape[1]), lambda i: (i, 0)),
                   pl.BlockSpec((tm, wg.shape[1]), lambda i: (i, 0))],
        out_shape=[jax.ShapeDtypeStruct((n, wr.shape[1]), F32),
                   jax.ShapeDtypeStruct((n, wm.shape[1]), F32),
                   jax.ShapeDtypeStruct((n, wg.shape[1]), BF16)],
        compiler_params=_cparams("arbitrary"),
        name="inproj",
    )(x2, sh, sc, g, wr, wm, wg)


def _rwkv_prep_kernel(z_ref, prev_ref, mu_ref, w0_ref, wdec_ref, a0_ref, wa_ref, wgu_ref,
                      kk_ref, ka_ref, rk_ref, ones_ref, tri_ref, blk_ref,
                      rp_ref, am_ref, bm_ref, km_ref, bh_ref, kh_ref, v_ref,
                      pc_ref, bv_ref, g_ref):
    tt = z_ref.shape[1]
    z = z_ref[0]
    prev = prev_ref[0][7:8, :]
    prev = jnp.where(pl.program_id(1) == 0, 0.0, prev)
    row = lax.broadcasted_iota(jnp.int32, (tt, 1), 0)
    zs = jnp.where(row == 0, prev, pltpu.roll(z, 1, axis=0))
    z = z + (zs - z) * mu_ref[...]
    zr = z[:, 0:D_RWKV]
    zk = z[:, D_RWKV:2 * D_RWKV]
    zv = z[:, 2 * D_RWKV:3 * D_RWKV]
    zwa = z[:, 3 * D_RWKV:3 * D_RWKV + DECAY_LORA + AAA_LORA]
    zg = z[:, 3 * D_RWKV + DECAY_LORA + AAA_LORA:]

    u = -(w0_ref[...] + _mm3(jnp.tanh(zwa), wdec_ref[...]))
    softplus = jnp.maximum(u, 0.0) + jnp.log(1.0 + jnp.exp(-jnp.abs(u)))
    logw = -jnp.exp(-softplus - 0.5)
    a = _sigmoid(a0_ref[...] + _mm(zwa, wa_ref[...]))
    g = _mm(_sigmoid(zg), wgu_ref[...])

    ones_blk = ones_ref[...]
    xk = zk * kk_ref[...]
    ss = _mm_rhs_exact(xk * xk, ones_blk)
    kk = xk * lax.rsqrt(jnp.maximum(ss, 1e-24))
    k = zk * (1.0 + (a - 1.0) * ka_ref[...])
    b = kk * a
    bonus = _mm_rhs_exact(zr * k * rk_ref[...], ones_blk)

    cum = _mm_lhs_exact(tri_ref[...], logw)
    tot = _mm_lhs_exact(blk_ref[...], logw)
    e_neg = jnp.exp(-cum)
    e_rem = jnp.exp(tot - cum)
    rp_ref[0] = (zr * jnp.exp(cum)).astype(BF16)
    am_ref[0] = (-kk * jnp.exp(cum - logw)).astype(BF16)
    bm_ref[0] = (b * e_neg).astype(BF16)
    km_ref[0] = (k * e_neg).astype(BF16)
    bh_ref[0] = (b * e_rem).astype(BF16)
    kh_ref[0] = (k * e_rem).astype(BF16)
    v_ref[0] = zv.astype(BF16)
    bv_ref[0] = bonus * zv
    g_ref[0] = g
    nc = tt // CHUNK
    pick = (lax.broadcasted_iota(jnp.int32, (nc, tt), 1)
            == CHUNK * lax.broadcasted_iota(jnp.int32, (nc, tt), 0))
    pc_ref[0] = _mm_lhs_exact(jnp.where(pick, 1.0, 0.0).astype(BF16), jnp.exp(tot))


def _rwkv_prep(zr3, mu, w0, wdec, a0, wa, wgu, k_k, k_a, r_k):
    bsz, seq, cols = zr3.shape
    tt = 512
    d = D_RWKV
    ones_blk = jnp.asarray(np.kron(np.eye(RWKV_HEADS), np.ones((RWKV_HEAD_DIM, RWKV_HEAD_DIM))), BF16)
    cidx = np.arange(tt) // CHUNK
    same = cidx[:, None] == cidx[None, :]
    tri = jnp.asarray(same & (np.arange(tt)[None, :] <= np.arange(tt)[:, None]), BF16)
    blk = jnp.asarray(same, BF16)
    tok = lambda b, j: (b, j, 0)
    big = pl.BlockSpec((1, tt, d), tok)
    out_bf = jax.ShapeDtypeStruct((bsz, seq, d), BF16)
    out_f = jax.ShapeDtypeStruct((bsz, seq, d), F32)
    return pl.pallas_call(
        _rwkv_prep_kernel,
        grid=(bsz, seq // tt),
        in_specs=[pl.BlockSpec((1, tt, cols), tok),
                  pl.BlockSpec((1, 8, cols), lambda b, j: (b, jnp.maximum(j * (tt // 8) - 1, 0), 0)),
                  _const_spec((1, cols)), _const_spec((1, d)), _const_spec(wdec.shape),
                  _const_spec((1, d)), _const_spec(wa.shape), _const_spec(wgu.shape),
                  _const_spec((1, d)), _const_spec((1, d)), _const_spec((1, d)),
                  _const_spec((d, d)), _const_spec((tt, tt)), _const_spec((tt, tt))],
        out_specs=[big] * 7 + [pl.BlockSpec((1, tt // CHUNK, d), tok), big, big],
        out_shape=[out_bf] * 7 + [jax.ShapeDtypeStruct((bsz, seq // CHUNK, d), F32), out_f, out_f],
        compiler_params=_cparams("arbitrary", "arbitrary"),
        name="rwkv_prep",
    )(zr3, zr3, mu, w0, wdec, a0, wa, wgu, k_k, k_a, r_k, ones_blk, tri, blk)


def _rwkv_scan_kernel(rp_ref, am_ref, bm_ref, km_ref, bh_ref, kh_ref, v_ref, pc_ref,
                      bv_ref, g_ref, lnw_ref, lnb_ref, o_ref, h_scr, y_scr):
    tb = rp_ref.shape[1]
    nc = tb // CHUNK
    pp = rp_ref.shape[2] // LANES
    first_pair = pl.program_id(2) * pp

    @pl.when(pl.program_id(1) == 0)
    def _():
        for p in range(pp):
            h_scr[first_pair + p] = jnp.zeros((LANES, LANES), F32)

    lane = lax.broadcasted_iota(jnp.int32, (1, LANES), 1)
    head_masks = [lane < RWKV_HEAD_DIM, lane >= RWKV_HEAD_DIM]
    r64 = lax.broadcasted_iota(jnp.int32, (CHUNK, CHUNK), 0)
    c64 = lax.broadcasted_iota(jnp.int32, (CHUNK, CHUNK), 1)
    eye64 = c64 == r64
    r2 = lax.broadcasted_iota(jnp.int32, (2 * CHUNK, CHUNK), 0)
    c2 = lax.broadcasted_iota(jnp.int32, (2 * CHUNK, CHUNK), 1)
    causal2 = ((r2 < CHUNK) & (c2 < r2)) | ((r2 >= CHUNK) & (c2 <= r2 - CHUNK))
    level_masks = [((r64 >> (k + 1)) == (c64 >> (k + 1))) & ((r64 >> k) == (c64 >> k) + 1)
                   for k in range(6)]
    r128 = lax.broadcasted_iota(jnp.int32, (LANES, LANES), 0)
    c128 = lax.broadcasted_iota(jnp.int32, (LANES, LANES), 1)
    blockdiag = (r128 < RWKV_HEAD_DIM) == (c128 < RWKV_HEAD_DIM)
    eye = r128 == c128
    zero_bf = jnp.zeros((), BF16)

    slabs = [(p, c) for p in range(pp) for c in range(nc)]
    items = [(p, c, h) for (p, c) in slabs for h in range(2)]

    def load(ref, p, c):
        return ref[0, pl.ds(c * CHUNK, CHUNK), pl.ds(p * LANES, LANES)]

    rp = {s: load(rp_ref, *s) for s in slabs}
    am = {s: load(am_ref, *s) for s in slabs}
    bm = {s: load(bm_ref, *s) for s in slabs}
    km = {s: load(km_ref, *s) for s in slabs}
    v = {s: load(v_ref, *s) for s in slabs}

    am_h, v_h, a_b, a_k = {}, {}, {}, {}
    for (p, c, h) in items:
        s = (p, c)
        am_h[p, c, h] = jnp.where(head_masks[h], am[s], zero_bf)
        v_h[p, c, h] = jnp.where(head_masks[h], v[s], zero_bf)
        lhs = jnp.concatenate([am_h[p, c, h], jnp.where(head_masks[h], rp[s], zero_bf)], axis=0)
        a_b[p, c, h] = jnp.where(causal2, _dot_nt(lhs, bm[s]), 0.0)
        a_k[p, c, h] = jnp.where(causal2, _dot_nt(lhs, km[s]), 0.0)
    akv = {i: _dot(a_k[i].astype(BF16), v_h[i]) for i in items}

    a_ab = {i: a_b[i][:CHUNK] for i in items}
    tinv = {i: jnp.where(eye64, 1.0, 0.0) + jnp.where(level_masks[0], a_ab[i], 0.0) for i in items}
    for lm in level_masks[1:]:
        tbf = {i: tinv[i].astype(BF16) for i in items}
        inner = {i: _dot(jnp.where(lm, a_ab[i], 0.0).astype(BF16), tbf[i]) for i in items}
        tinv = {i: tinv[i] + _dot(tbf[i], inner[i].astype(BF16)) for i in items}
    zc = {i: _dot(tinv[i].astype(BF16),
                  jnp.concatenate([am_h[i], akv[i][:CHUNK].astype(BF16)], axis=1)) for i in items}
    gyc = {i: _dot(a_b[i][CHUNK:].astype(BF16), zc[i].astype(BF16)) for i in items}

    gy, m, hadd = {}, {}, {}
    for s in slabs:
        p, c = s
        z_sum = (zc[p, c, 0] + zc[p, c, 1]).astype(BF16)
        mh = _dot_tn(load(bh_ref, p, c), z_sum)
        kv = _dot_tn(load(kh_ref, p, c), v[s])
        pc = pc_ref[0, c:c + 1, pl.ds(p * LANES, LANES)]
        m[s] = (jnp.where(blockdiag, mh[:, :LANES], 0.0) + jnp.where(eye, pc, 0.0)).astype(BF16)
        hadd[s] = jnp.where(blockdiag, mh[:, LANES:] + kv, 0.0)
        y0 = akv[p, c, 0][CHUNK:] + akv[p, c, 1][CHUNK:]
        g2 = gyc[p, c, 0] + gyc[p, c, 1]
        gy[s] = ((rp[s].astype(F32) + g2[:, :LANES]).astype(BF16), g2[:, LANES:] + y0)

    hs = [h_scr[first_pair + p] for p in range(pp)]
    for c in range(nc):
        for p in range(pp):
            hb = hs[p].astype(BF16)
            gmat, y0 = gy[p, c]
            y_scr[pl.ds(c * CHUNK, CHUNK), pl.ds(p * LANES, LANES)] = _dot(gmat, hb) + y0
            hs[p] = _dot(m[p, c], hb) + hadd[p, c]
    for p in range(pp):
        h_scr[first_pair + p] = hs[p]

    avg = jnp.where(blockdiag, 1.0 / RWKV_HEAD_DIM, 0.0).astype(BF16)
    for p in range(pp):
        cols = pl.ds(p * LANES, LANES)
        y = y_scr[:, cols]
        mean = _mm_rhs_exact(y, avg)
        yc = y - mean
        var = _mm_rhs_exact(yc * yc, avg)
        yn = yc * lax.rsqrt(var + LN_X_EPS)
        out = (yn * lnw_ref[:, cols] + lnb_ref[:, cols] + bv_ref[0, :, cols]) * g_ref[0, :, cols]
        o_ref[0, :, cols] = out.astype(BF16)


def _rwkv_scan(rp, am, bm, km, bh, kh, v, pc, bv, g, lnw, lnb):
    bsz, seq, d = rp.shape
    tb = 512
    npairs = d // LANES
    pp = 2
    width = pp * LANES
    tok = pl.BlockSpec((1, tb, width), lambda b, t, p: (b, t, p))
    vec = pl.BlockSpec((1, width), lambda b, t, p: (0, p))
    return pl.pallas_call(
        _rwkv_scan_kernel,
        grid=(bsz, seq // tb, npairs // pp),
        in_specs=[tok] * 7 + [pl.BlockSpec((1, tb // CHUNK, width), lambda b, t, p: (b, t, p)),
                              tok, tok, vec, vec],
        out_specs=tok,
        out_shape=jax.ShapeDtypeStruct((bsz, seq, d), BF16),
        scratch_shapes=[pltpu.VMEM((npairs, LANES, LANES), F32),
                        pltpu.VMEM((tb, width), F32)],
        compiler_params=_cparams("arbitrary", "arbitrary", "arbitrary"),
        name="rwkv_scan",
    )(rp, am, bm, km, bh, kh, v, pc, bv, g, lnw, lnb)


def _rope(x, cos, sin, lane):
    lo = QK_NOPE_DIM + QK_ROPE_DIM // 2
    partner = jnp.where(lane < lo,
                        -pltpu.roll(x, LANES - QK_ROPE_DIM // 2, axis=1),
                        pltpu.roll(x, QK_ROPE_DIM // 2, axis=1))
    return x * cos + partner * sin


def _mla_prep_kernel(z_ref, pos_ref, freq_ref, gq_ref, wuq_ref, gkv_ref, wukv_ref,
                     qn_ref, kn_ref, q_ref, k_ref, vt_ref):
    z = z_ref[0]
    zq = z[:, :Q_LORA]
    zkv = z[:, Q_LORA:Q_LORA + KV_LORA]
    zpe = z[:, Q_LORA + KV_LORA:]
    ang = pos_ref[0] * freq_ref[...]
    cos = jnp.cos(ang)
    sin = jnp.sin(ang)
    lane = lax.broadcasted_iota(jnp.int32, (1, LANES), 1)
    nope = lane < QK_NOPE_DIM

    qa = zq * lax.rsqrt(jnp.mean(zq * zq, axis=-1, keepdims=True) + EPS) * gq_ref[...]
    q_all = _mm(qa, wuq_ref[...])
    kva = zkv * lax.rsqrt(jnp.mean(zkv * zkv, axis=-1, keepdims=True) + EPS) * gkv_ref[...]
    kv_all = _mm(kva, wukv_ref[...])
    pe = pltpu.roll(zpe, QK_NOPE_DIM, axis=1)
    scale = QK_HEAD_DIM ** -0.5 * LOG2_E
    for h in range(MLA_HEADS):
        q = q_all[:, h * HEAD_PAD:(h + 1) * HEAD_PAD]
        ms = jnp.sum(q * q, axis=-1, keepdims=True) * (1.0 / QK_HEAD_DIM)
        q = q * lax.rsqrt(ms + EPS) * qn_ref[...]
        q = _rope(q, cos, sin, lane) * scale
        q_ref[0, h] = q.astype(BF16)
        kv = kv_all[:, h * HEAD_PAD:(h + 1) * HEAD_PAD]
        k = jnp.where(nope, kv, pe)
        ms = jnp.sum(k * k, axis=-1, keepdims=True) * (1.0 / QK_HEAD_DIM)
        k = k * lax.rsqrt(ms + EPS) * kn_ref[...]
        k_ref[0, h] = _rope(k, cos, sin, lane).astype(BF16)
        vv = pltpu.roll(kv, LANES - QK_NOPE_DIM, axis=1)
        vv = jnp.where(nope, vv, jnp.where(lane == V_HEAD_DIM, 1.0, 0.0))
        vt_ref[0, h] = vv.T.astype(BF16)


def _mla_prep(zm3, pos, freq, gq, wuq, gkv, wukv, qn, kn):
    bsz, seq, cols = zm3.shape
    tm = 512
    tok = lambda b, j: (b, j, 0)
    head = lambda b, j: (b, 0, j, 0)
    return pl.pallas_call(
        _mla_prep_kernel,
        grid=(bsz, seq // tm),
        in_specs=[pl.BlockSpec((1, tm, cols), tok), pl.BlockSpec((1, tm, 1), tok),
                  _const_spec((1, LANES)), _const_spec((1, Q_LORA)), _const_spec(wuq.shape),
                  _const_spec((1, KV_LORA)), _const_spec(wukv.shape),
                  _const_spec((1, LANES)), _const_spec((1, LANES))],
        out_specs=[pl.BlockSpec((1, MLA_HEADS, tm, HEAD_PAD), head),
                   pl.BlockSpec((1, MLA_HEADS, tm, HEAD_PAD), head),
                   pl.BlockSpec((1, MLA_HEADS, HEAD_PAD, tm), lambda b, j: (b, 0, 0, j))],
        out_shape=[jax.ShapeDtypeStruct((bsz, MLA_HEADS, seq, HEAD_PAD), BF16),
                   jax.ShapeDtypeStruct((bsz, MLA_HEADS, seq, HEAD_PAD), BF16),
                   jax.ShapeDtypeStruct((bsz, MLA_HEADS, HEAD_PAD, seq), BF16)],
        compiler_params=_cparams("arbitrary", "arbitrary"),
        name="mla_prep",
    )(zm3, pos, freq, gq, wuq, gkv, wukv, qn, kn)


def _attn_kernel(qi_ref, kj_ref, q_ref, k_ref, vt_ref, o_ref, m_scr, acc_scr):
    p = pl.program_id(1)
    qi = qi_ref[p]
    kj = kj_ref[p]
    bq = q_ref.shape[2]
    bk = k_ref.shape[2]
    heads = range(MLA_HEADS)

    @pl.when(kj == 0)
    def _():
        m_scr[...] = jnp.full(m_scr.shape, NEG_BIG, F32)
        acc_scr[...] = jnp.zeros(acc_scr.shape, F32)

    def block(diagonal):
        st = [_dot_nt(k_ref[0, h], q_ref[0, h]) for h in heads]
        if diagonal:
            visible = (lax.broadcasted_iota(jnp.int32, (bk, bq), 0)
                       <= lax.broadcasted_iota(jnp.int32, (bk, bq), 1))
            st = [jnp.where(visible, x, NEG_BIG) for x in st]
        m_prev = [m_scr[h] for h in heads]
        m_new = [jnp.maximum(m_prev[h], jnp.max(st[h], axis=0, keepdims=True)) for h in heads]
        pr = [jnp.exp2(st[h] - m_new[h]).astype(BF16) for h in heads]
        alpha = [jnp.exp2(m_prev[h] - m_new[h]) for h in heads]
        pv = [_dot(vt_ref[0, h], pr[h]) for h in heads]
        for h in heads:
            acc_scr[h] = alpha[h] * acc_scr[h] + pv[h]
            m_scr[h] = m_new[h]

    @pl.when(kj != qi)
    def _():
        block(False)

    @pl.when(kj == qi)
    def _():
        block(True)
        for h in heads:
            acc = acc_scr[h]
            o = acc[:V_HEAD_DIM] * (1.0 / acc[V_HEAD_DIM:V_HEAD_DIM + 1])
            o_ref[0, h * V_HEAD_DIM:(h + 1) * V_HEAD_DIM, :] = o.astype(BF16)


def _attn(q, k, v):
    bsz, nh, seq, _ = q.shape
    bq = bk = 512
    nq = seq // bq
    qi = np.concatenate([np.full(i + 1, i) for i in range(nq)]).astype(np.int32)
    kj = np.concatenate([np.arange(i + 1) for i in range(nq)]).astype(np.int32)
    grid_spec = pltpu.PrefetchScalarGridSpec(
        num_scalar_prefetch=2,
        grid=(bsz, len(qi)),
        in_specs=[pl.BlockSpec((1, nh, bq, HEAD_PAD), lambda b, p, qi, kj: (b, 0, qi[p], 0)),
                  pl.BlockSpec((1, nh, bk, HEAD_PAD), lambda b, p, qi, kj: (b, 0, kj[p], 0)),
                  pl.BlockSpec((1, nh, HEAD_PAD, bk), lambda b, p, qi, kj: (b, 0, 0, kj[p]))],
        out_specs=pl.BlockSpec((1, nh * V_HEAD_DIM, bq), lambda b, p, qi, kj: (b, 0, qi[p])),
        scratch_shapes=[pltpu.VMEM((nh, 1, bq), F32), pltpu.VMEM((nh, HEAD_PAD, bq), F32)],
    )
    return pl.pallas_call(
        _attn_kernel,
        grid_spec=grid_spec,
        out_shape=jax.ShapeDtypeStruct((bsz, nh * V_HEAD_DIM, seq), BF16),
        compiler_params=_cparams("arbitrary", "arbitrary"),
        name="mla_attn",
    )(jnp.asarray(qi), jnp.asarray(kj), q, k, v)


def _first_index_of_max(vals, lane):
    mx = jnp.max(vals, axis=-1, keepdims=True)
    idx = jnp.min(jnp.where(vals == mx, lane, LANES), axis=-1, keepdims=True)
    return mx, idx


def _post_kernel(yr_ref, ymt_ref, sg_ref, x_ref, gt_ref, sh_ref, sc_ref, g_ref,
                 wb0_ref, wb1_ref, wo_ref, wr_ref, br_ref, x1_ref, h2_ref, cmb_ref):
    sg = sg_ref[...].astype(F32)
    merged = (sg[:, :D_MODEL] * _dot(yr_ref[...], wb0_ref[...])
              + sg[:, D_MODEL:] * _dot_tn(ymt_ref[0], wb1_ref[...]))
    x1 = x_ref[...] + gt_ref[0] * _mm(merged, wo_ref[...])
    x1_ref[...] = x1
    ms = jnp.mean(x1 * x1, axis=-1, keepdims=True)
    h2 = x1 * lax.rsqrt(ms + EPS) * g_ref[...]
    h2 = h2 * (1.0 + sc_ref[0]) + sh_ref[0]
    h2_ref[...] = h2.astype(BF16)

    logits = _mm3(h2, wr_ref[...]) + br_ref[...]
    lane = lax.broadcasted_iota(jnp.int32, logits.shape, 1).astype(F32)
    gl = jnp.where((lane >= N_EXPERTS) & (lane < N_EXPERTS + N_GROUPS), logits, NEG_BIG)
    gmax, gidx = _first_index_of_max(gl, lane)
    p_g = 1.0 / jnp.sum(jnp.exp(gl - gmax), axis=-1, keepdims=True)
    first = (gidx - N_EXPERTS) * EXPERTS_PER_GROUP
    el = jnp.where((lane >= first) & (lane < first + EXPERTS_PER_GROUP), logits, NEG_BIG)
    m1, i1 = _first_index_of_max(el, lane)
    z = jnp.sum(jnp.exp(el - m1), axis=-1, keepdims=True)
    el2 = jnp.where(lane == i1, NEG_BIG, el)
    m2, i2 = _first_index_of_max(el2, lane)
    p1 = 1.0 / z
    p2 = jnp.exp(m2 - m1) / z
    tot = p1 + p2
    cmb_ref[...] = (jnp.where(lane == i1, p1 / tot * p_g, 0.0)
                    + jnp.where(lane == i2, p2 / tot * p_g, 0.0))


def _post(yr, ym, sg, x2, gt, sh, sc, g, wb0, wb1, wo, wr, br, seq):
    n, d = x2.shape
    tm = 512
    per_b = seq // tm
    bmap = lambda i: (i // per_b, 0, 0)
    rowmap = lambda i: (i, 0)
    return pl.pallas_call(
        _post_kernel,
        grid=(n // tm,),
        in_specs=[pl.BlockSpec((tm, D_RWKV), rowmap),
                  pl.BlockSpec((1, D_MLA, tm), lambda i: (i // per_b, 0, i % per_b)),
                  pl.BlockSpec((tm, 2 * d), rowmap), pl.BlockSpec((tm, d), rowmap),
                  pl.BlockSpec((1, 1, d), bmap), pl.BlockSpec((1, 1, d), bmap),
                  pl.BlockSpec((1, 1, d), bmap), _const_spec((1, d)),
                  _const_spec(wb0.shape), _const_spec(wb1.shape), _const_spec(wo.shape),
                  _const_spec(wr.shape), _const_spec(br.shape)],
        out_specs=[pl.BlockSpec((tm, d), rowmap), pl.BlockSpec((tm, d), rowmap),
                   pl.BlockSpec((tm, LANES), rowmap)],
        out_shape=[jax.ShapeDtypeStruct((n, d), F32), jax.ShapeDtypeStruct((n, d), BF16),
                   jax.ShapeDtypeStruct((n, LANES), F32)],
        compiler_params=_cparams("arbitrary"),
        name="post",
    )(yr, ym, sg, x2, gt, sh, sc, g, wb0, wb1, wo, wr, br)


def _moe_kernel(h_ref, cmb_ref, x1_ref, gt_ref, wg_ref, wu_ref, wd_ref, o_ref, acc_scr):
    e = pl.program_id(1)

    @pl.when(e == 0)
    def _():
        acc_scr[...] = jnp.zeros(acc_scr.shape, F32)

    h = h_ref[...]
    gate = _dot(h, wg_ref[0])
    hid = gate * _sigmoid(gate) * _dot(h, wu_ref[0])
    cmb = cmb_ref[...]
    lane = lax.broadcasted_iota(jnp.int32, cmb.shape, 1)
    cw = jnp.sum(jnp.where(lane == e, cmb, 0.0), axis=-1, keepdims=True)
    acc_scr[...] += _mm(hid * cw, wd_ref[0])

    @pl.when(e == pl.num_programs(1) - 1)
    def _():
        o_ref[...] = x1_ref[...] + gt_ref[0] * acc_scr[...]


def _moe(h2, cmb, x1, gt, wg, wu, wd, seq):
    n, d = x1.shape
    ne = wg.shape[0]
    tm = 1024
    per_b = seq // tm
    rowmap = lambda i, e: (i, 0)
    return pl.pallas_call(
        _moe_kernel,
        grid=(n // tm, ne),
        in_specs=[pl.BlockSpec((tm, d), rowmap), pl.BlockSpec((tm, LANES), rowmap),
                  pl.BlockSpec((tm, d), rowmap),
                  pl.BlockSpec((1, 1, d), lambda i, e: (i // per_b, 0, 0)),
                  pl.BlockSpec((1, d, D_EXPERT), lambda i, e: (e, 0, 0)),
                  pl.BlockSpec((1, d, D_EXPERT), lambda i, e: (e, 0, 0)),
                  pl.BlockSpec((1, D_EXPERT, d), lambda i, e: (e, 0, 0))],
        out_specs=pl.BlockSpec((tm, d), rowmap),
        out_shape=jax.ShapeDtypeStruct((n, d), F32),
        scratch_shapes=[pltpu.VMEM((tm, d), F32)],
        compiler_params=_cparams("arbitrary", "arbitrary"),
        name="moe",
    )(h2, cmb, x1, gt, wg, wu, wd)


def _pad_cols(w, n):
    return jnp.pad(w, ((0, 0), (0, n - w.shape[1])))


def kernel(x, c, positions, w_ada, b_ada, g_norm_mix, w_in, mu_shift, w0, w_decay_up, a0, w_a_up, w_g_up, k_k, k_a, r_k, ln_x_w, ln_x_b, g_q_a, w_uq, g_kv_a, w_ukv, q_norm, k_norm, w_branch, w_out, g_norm_ffn, w_router_group, b_router_group, w_router_expert, b_router_expert, w_e_gate, w_e_up, w_e_down):
    bsz, seq, d = x.shape
    n = bsz * seq
    row = lambda a: a.reshape(1, -1)

    freqs = ROPE_THETA ** (-(jnp.arange(0, QK_ROPE_DIM, 2, dtype=F32) / QK_ROPE_DIM))
    freq = jnp.zeros((1, LANES), F32).at[0, QK_NOPE_DIM:QK_NOPE_DIM + QK_ROPE_DIM].set(
        jnp.concatenate([freqs, freqs]))
    pos = positions.astype(F32).reshape(bsz, seq, 1)

    x2 = x.reshape(n, d)
    for l in range(w_ada.shape[0]):
        mod = _ada(c, w_ada[l], b_ada[l])
        sh_m, sc_m, gt_m, sh_f, sc_f, gt_f = [m.reshape(bsz, 1, d) for m in jnp.split(mod, 6, axis=-1)]

        w_in_l = w_in[l].astype(BF16)
        wr = w_in_l[:, :RWKV_COLS]
        wm = _pad_cols(w_in_l[:, RWKV_COLS:RWKV_COLS + MLA_COLS], MLA_COLS_PAD)
        wg = w_in_l[:, RWKV_COLS + MLA_COLS:]
        zr, zm, sg = _inproj(x2, sh_m, sc_m, row(g_norm_mix[l]), wr, wm, wg, seq)

        zeros_lora = jnp.zeros((DECAY_LORA, D_RWKV), F32)
        wdec = jnp.concatenate([w_decay_up[l], zeros_lora], axis=0)
        wa = jnp.concatenate([zeros_lora, w_a_up[l]], axis=0)
        prep = _rwkv_prep(zr.reshape(bsz, seq, RWKV_COLS), row(mu_shift[l]), row(w0[l]), wdec,
                          row(a0[l]), wa, w_g_up[l], row(k_k[l]), row(k_a[l]), row(r_k[l]))
        y_rwkv = _rwkv_scan(*prep, row(ln_x_w[l]), row(ln_x_b[l]))

        wuq = w_uq[l].reshape(Q_LORA, MLA_HEADS, QK_HEAD_DIM)
        wuq = jnp.pad(wuq, ((0, 0), (0, 0), (0, HEAD_PAD - QK_HEAD_DIM))).reshape(Q_LORA, -1)
        qn = _pad_cols(row(q_norm[l]), LANES)
        kn = _pad_cols(row(k_norm[l]), LANES)
        q, k, v = _mla_prep(zm.reshape(bsz, seq, MLA_COLS_PAD), pos, freq, row(g_q_a[l]),
                            wuq.astype(BF16), row(g_kv_a[l]), w_ukv[l].astype(BF16), qn, kn)
        y_mla = _attn(q, k, v)

        w_router = _pad_cols(jnp.concatenate([w_router_expert[l], w_router_group[l]], axis=1), LANES)
        b_router = _pad_cols(row(jnp.concatenate([b_router_expert[l], b_router_group[l]])), LANES)
        x1, h2, cmb = _post(y_rwkv.reshape(n, D_RWKV), y_mla, sg, x2,
                            gt_m, sh_f, sc_f, row(g_norm_ffn[l]),
                            w_branch[l, 0].astype(BF16), w_branch[l, 1].astype(BF16),
                            w_out[l].astype(BF16), w_router, b_router, seq)

        x2 = _moe(h2, cmb, x1, gt_f, w_e_gate[l].astype(BF16), w_e_up[l].astype(BF16),
                  w_e_down[l].astype(BF16), seq)
    return x2.reshape(bsz, seq, d)
```

```python
import functools

import numpy as np
import jax
import jax.numpy as jnp
from jax import lax
from jax.experimental import pallas as pl
from jax.experimental.pallas import tpu as pltpu

F32 = jnp.float32
BF16 = jnp.bfloat16

D_MODEL = 1024
EPS = 1e-6
RWKV_HEADS = 8
RWKV_HEAD_DIM = 64
D_RWKV = 512
DECAY_LORA = 64
AAA_LORA = 64
GATE_LORA = 128
LN_X_EPS = 64e-5
MLA_HEADS = 8
QK_NOPE_DIM = 64
QK_ROPE_DIM = 32
QK_HEAD_DIM = 96
V_HEAD_DIM = 64
D_MLA = 512
Q_LORA = 256
KV_LORA = 128
ROPE_THETA = 10000.0
RWKV_COLS = 1792
MLA_COLS = 416
MLA_COLS_PAD = 512
N_GROUPS = 4
EXPERTS_PER_GROUP = 8
N_EXPERTS = 32
D_EXPERT = 256

LANES = 128
CHUNK = 64
HEAD_PAD = 128
NEG_BIG = -1e30
LOG2_E = 1.4426950408889634
MAX_FIXED_SHIFT_SPAN = 80.0
VMEM_LIMIT = 56 * 1024 * 1024


def _cparams(*sem):
    return pltpu.CompilerParams(dimension_semantics=sem, vmem_limit_bytes=VMEM_LIMIT)


def _dot(a, b):
    return jnp.dot(a, b, preferred_element_type=F32)


def _dot_nt(a, b):
    return lax.dot_general(a, b, (((1,), (1,)), ((), ())), preferred_element_type=F32)


def _dot_tn(a, b):
    return lax.dot_general(a, b, (((0,), (0,)), ((), ())), preferred_element_type=F32)


def _mm(a, b):
    return _dot(a.astype(BF16), b.astype(BF16))


def _split(a):
    hi = a.astype(BF16)
    lo = (a - hi.astype(F32)).astype(BF16)
    return hi, lo


def _mm_rhs_exact(a, b):
    hi, lo = _split(a)
    return _dot(hi, b) + _dot(lo, b)


def _mm_lhs_exact(a, b):
    hi, lo = _split(b)
    return _dot(a, hi) + _dot(a, lo)


def _mm3(a, b):
    ah, al = _split(a)
    bh, bl = _split(b)
    return _dot(ah, bh) + (_dot(ah, bl) + _dot(al, bh))


def _sigmoid(x):
    return 1.0 / (1.0 + jnp.exp(-x))


def _const_spec(shape):
    nd = len(shape)
    return pl.BlockSpec(shape, lambda *_: (0,) * nd)


def _ada_kernel(c_ref, w_ref, b_ref, o_ref):
    c = c_ref[...]
    s = c * _sigmoid(c)
    o_ref[...] = _mm3(s, w_ref[...]) + b_ref[...]


def _ada(c, w, b):
    bsz, d = c.shape
    n = w.shape[1]
    tn = 1024
    return pl.pallas_call(
        _ada_kernel,
        grid=(n // tn,),
        in_specs=[pl.BlockSpec((bsz, d), lambda j: (0, 0)),
                  pl.BlockSpec((d, tn), lambda j: (0, j)),
                  pl.BlockSpec((1, tn), lambda j: (0, j))],
        out_specs=pl.BlockSpec((bsz, tn), lambda j: (0, j)),
        out_shape=jax.ShapeDtypeStruct((bsz, n), F32),
        compiler_params=_cparams("arbitrary"),
        name="ada",
    )(c, w, b.reshape(1, n))


def _inproj_kernel(x_ref, sh_ref, sc_ref, g_ref, wr_ref, wm_ref, wg_ref,
                   zr_ref, zm_ref, sg_ref):
    x = x_ref[...]
    ms = jnp.mean(x * x, axis=-1, keepdims=True)
    h = x * lax.rsqrt(ms + EPS) * g_ref[...]
    h = h * (1.0 + sc_ref[0]) + sh_ref[0]
    hb = h.astype(BF16)
    zr_ref[...] = _dot(hb, wr_ref[...])
    zm_ref[...] = _dot(hb, wm_ref[...])
    sg_ref[...] = _sigmoid(_dot(hb, wg_ref[...])).astype(BF16)


def _inproj(x2, sh, sc, g, wr, wm, wg, seq):
    n, d = x2.shape
    tm = 512
    per_b = seq // tm
    bmap = lambda i: (i // per_b, 0, 0)
    return pl.pallas_call(
        _inproj_kernel,
        grid=(n // tm,),
        in_specs=[pl.BlockSpec((tm, d), lambda i: (i, 0)),
                  pl.BlockSpec((1, 1, d), bmap),
                  pl.BlockSpec((1, 1, d), bmap),
                  _const_spec((1, d)),
                  _const_spec(wr.shape), _const_spec(wm.shape), _const_spec(wg.shape)],
        out_specs=[pl.BlockSpec((tm, wr.shape[1]), lambda i: (i, 0)),
                   pl.BlockSpec((tm, wm.shape[1]), lambda i: (i, 0)),
                   pl.BlockSpec((tm, wg.shape[1]), lambda i: (i, 0))],
        out_shape=[jax.ShapeDtypeStruct((n, wr.shape[1]), F32),
                   jax.ShapeDtypeStruct((n, wm.shape[1]), F32),
                   jax.ShapeDtypeStruct((n, wg.shape[1]), BF16)],
        compiler_params=_cparams("arbitrary"),
        name="inproj",
    )(x2, sh, sc, g, wr, wm, wg)


def _rwkv_prep_kernel(z_ref, prev_ref, mu_ref, w0_ref, wdec_ref, a0_ref, wa_ref, wgu_ref,
                      kk_ref, ka_ref, rk_ref, ones_ref, tri_ref, blk_ref,
                      rp_ref, am_ref, bm_ref, km_ref, bh_ref, kh_ref, v_ref,
                      pc_ref, bv_ref, g_ref):
    tt = z_ref.shape[1]
    z = z_ref[0]
    prev = prev_ref[0][7:8, :]
    prev = jnp.where(pl.program_id(1) == 0, 0.0, prev)
    row = lax.broadcasted_iota(jnp.int32, (tt, 1), 0)
    zs = jnp.where(row == 0, prev, pltpu.roll(z, 1, axis=0))
    z = z + (zs - z) * mu_ref[...]
    zr = z[:, 0:D_RWKV]
    zk = z[:, D_RWKV:2 * D_RWKV]
    zv = z[:, 2 * D_RWKV:3 * D_RWKV]
    zwa = z[:, 3 * D_RWKV:3 * D_RWKV + DECAY_LORA + AAA_LORA]
    zg = z[:, 3 * D_RWKV + DECAY_LORA + AAA_LORA:]

    u = -(w0_ref[...] + _mm3(jnp.tanh(zwa), wdec_ref[...]))
    softplus = jnp.maximum(u, 0.0) + jnp.log(1.0 + jnp.exp(-jnp.abs(u)))
    logw = -jnp.exp(-softplus - 0.5)
    a = _sigmoid(a0_ref[...] + _mm(zwa, wa_ref[...]))
    g = _mm(_sigmoid(zg), wgu_ref[...])

    ones_blk = ones_ref[...]
    xk = zk * kk_ref[...]
    ss = _mm_rhs_exact(xk * xk, ones_blk)
    kk = xk * lax.rsqrt(jnp.maximum(ss, 1e-24))
    k = zk * (1.0 + (a - 1.0) * ka_ref[...])
    b = kk * a
    bonus = _mm_rhs_exact(zr * k * rk_ref[...], ones_blk)

    cum = _mm_lhs_exact(tri_ref[...], logw)
    tot = _mm_lhs_exact(blk_ref[...], logw)
    e_neg = jnp.exp(-cum)
    e_rem = jnp.exp(tot - cum)
    rp_ref[0] = (zr * jnp.exp(cum)).astype(BF16)
    am_ref[0] = (-kk * jnp.exp(cum - logw)).astype(BF16)
    bm_ref[0] = (b * e_neg).astype(BF16)
    km_ref[0] = (k * e_neg).astype(BF16)
    bh_ref[0] = (b * e_rem).astype(BF16)
    kh_ref[0] = (k * e_rem).astype(BF16)
    v_ref[0] = zv.astype(BF16)
    bv_ref[0] = bonus * zv
    g_ref[0] = g
    nc = tt // CHUNK
    pick = (lax.broadcasted_iota(jnp.int32, (nc, tt), 1)
            == CHUNK * lax.broadcasted_iota(jnp.int32, (nc, tt), 0))
    pc_ref[0] = _mm_lhs_exact(jnp.where(pick, 1.0, 0.0).astype(BF16), jnp.exp(tot))


def _rwkv_prep(zr3, mu, w0, wdec, a0, wa, wgu, k_k, k_a, r_k):
    bsz, seq, cols = zr3.shape
    tt = 512
    d = D_RWKV
    ones_blk = jnp.asarray(np.kron(np.eye(RWKV_HEADS), np.ones((RWKV_HEAD_DIM, RWKV_HEAD_DIM))), BF16)
    cidx = np.arange(tt) // CHUNK
    same = cidx[:, None] == cidx[None, :]
    tri = jnp.asarray(same & (np.arange(tt)[None, :] <= np.arange(tt)[:, None]), BF16)
    blk = jnp.asarray(same, BF16)
    tok = lambda b, j: (b, j, 0)
    big = pl.BlockSpec((1, tt, d), tok)
    out_bf = jax.ShapeDtypeStruct((bsz, seq, d), BF16)
    out_f = jax.ShapeDtypeStruct((bsz, seq, d), F32)
    return pl.pallas_call(
        _rwkv_prep_kernel,
        grid=(bsz, seq // tt),
        in_specs=[pl.BlockSpec((1, tt, cols), tok),
                  pl.BlockSpec((1, 8, cols), lambda b, j: (b, jnp.maximum(j * (tt // 8) - 1, 0), 0)),
                  _const_spec((1, cols)), _const_spec((1, d)), _const_spec(wdec.shape),
                  _const_spec((1, d)), _const_spec(wa.shape), _const_spec(wgu.shape),
                  _const_spec((1, d)), _const_spec((1, d)), _const_spec((1, d)),
                  _const_spec((d, d)), _const_spec((tt, tt)), _const_spec((tt, tt))],
        out_specs=[big] * 7 + [pl.BlockSpec((1, tt // CHUNK, d), tok), big, big],
        out_shape=[out_bf] * 7 + [jax.ShapeDtypeStruct((bsz, seq // CHUNK, d), F32), out_f, out_f],
        compiler_params=_cparams("arbitrary", "arbitrary"),
        name="rwkv_prep",
    )(zr3, zr3, mu, w0, wdec, a0, wa, wgu, k_k, k_a, r_k, ones_blk, tri, blk)


def _rwkv_scan_kernel(rp_ref, am_ref, bm_ref, km_ref, bh_ref, kh_ref, v_ref, pc_ref,
                      bv_ref, g_ref, lnw_ref, lnb_ref, o_ref, h_scr, y_scr):
    tb = rp_ref.shape[1]
    nc = tb // CHUNK
    pp = rp_ref.shape[2] // LANES
    first_pair = pl.program_id(2) * pp

    @pl.when(pl.program_id(1) == 0)
    def _():
        for p in range(pp):
            h_scr[first_pair + p] = jnp.zeros((LANES, LANES), F32)

    lane = lax.broadcasted_iota(jnp.int32, (1, LANES), 1)
    head_masks = [lane < RWKV_HEAD_DIM, lane >= RWKV_HEAD_DIM]
    r64 = lax.broadcasted_iota(jnp.int32, (CHUNK, CHUNK), 0)
    c64 = lax.broadcasted_iota(jnp.int32, (CHUNK, CHUNK), 1)
    eye64 = c64 == r64
    r2 = lax.broadcasted_iota(jnp.int32, (2 * CHUNK, CHUNK), 0)
    c2 = lax.broadcasted_iota(jnp.int32, (2 * CHUNK, CHUNK), 1)
    causal2 = ((r2 < CHUNK) & (c2 < r2)) | ((r2 >= CHUNK) & (c2 <= r2 - CHUNK))
    level_masks = [((r64 >> (k + 1)) == (c64 >> (k + 1))) & ((r64 >> k) == (c64 >> k) + 1)
                   for k in range(6)]
    r128 = lax.broadcasted_iota(jnp.int32, (LANES, LANES), 0)
    c128 = lax.broadcasted_iota(jnp.int32, (LANES, LANES), 1)
    blockdiag = (r128 < RWKV_HEAD_DIM) == (c128 < RWKV_HEAD_DIM)
    eye = r128 == c128
    zero_bf = jnp.zeros((), BF16)

    slabs = [(p, c) for p in range(pp) for c in range(nc)]
    items = [(p, c, h) for (p, c) in slabs for h in range(2)]

    def load(ref, p, c):
        return ref[0, pl.ds(c * CHUNK, CHUNK), pl.ds(p * LANES, LANES)]

    rp = {s: load(rp_ref, *s) for s in slabs}
    am = {s: load(am_ref, *s) for s in slabs}
    bm = {s: load(bm_ref, *s) for s in slabs}
    km = {s: load(km_ref, *s) for s in slabs}
    v = {s: load(v_ref, *s) for s in slabs}

    am_h, v_h, a_b, a_k = {}, {}, {}, {}
    for (p, c, h) in items:
        s = (p, c)
        am_h[p, c, h] = jnp.where(head_masks[h], am[s], zero_bf)
        v_h[p, c, h] = jnp.where(head_masks[h], v[s], zero_bf)
        lhs = jnp.concatenate([am_h[p, c, h], jnp.where(head_masks[h], rp[s], zero_bf)], axis=0)
        a_b[p, c, h] = jnp.where(causal2, _dot_nt(lhs, bm[s]), 0.0)
        a_k[p, c, h] = jnp.where(causal2, _dot_nt(lhs, km[s]), 0.0)
    akv = {i: _dot(a_k[i].astype(BF16), v_h[i]) for i in items}

    a_ab = {i: a_b[i][:CHUNK] for i in items}
    tinv = {i: jnp.where(eye64, 1.0, 0.0) + jnp.where(level_masks[0], a_ab[i], 0.0) for i in items}
    for lm in level_masks[1:]:
        tbf = {i: tinv[i].astype(BF16) for i in items}
        inner = {i: _dot(jnp.where(lm, a_ab[i], 0.0).astype(BF16), tbf[i]) for i in items}
        tinv = {i: tinv[i] + _dot(tbf[i], inner[i].astype(BF16)) for i in items}
    zc = {i: _dot(tinv[i].astype(BF16),
                  jnp.concatenate([am_h[i], akv[i][:CHUNK].astype(BF16)], axis=1)) for i in items}
    gyc = {i: _dot(a_b[i][CHUNK:].astype(BF16), zc[i].astype(BF16)) for i in items}

    gy, m, hadd = {}, {}, {}
    for s in slabs:
        p, c = s
        z_sum = (zc[p, c, 0] + zc[p, c, 1]).astype(BF16)
        mh = _dot_tn(load(bh_ref, p, c), z_sum)
        kv = _dot_tn(load(kh_ref, p, c), v[s])
        pc = pc_ref[0, c:c + 1, pl.ds(p * LANES, LANES)]
        m[s] = (jnp.where(blockdiag, mh[:, :LANES], 0.0) + jnp.where(eye, pc, 0.0)).astype(BF16)
        hadd[s] = jnp.where(blockdiag, mh[:, LANES:] + kv, 0.0)
        y0 = akv[p, c, 0][CHUNK:] + akv[p, c, 1][CHUNK:]
        g2 = gyc[p, c, 0] + gyc[p, c, 1]
        gy[s] = ((rp[s].astype(F32) + g2[:, :LANES]).astype(BF16), g2[:, LANES:] + y0)

    hs = [h_scr[first_pair + p] for p in range(pp)]
    for c in range(nc):
        for p in range(pp):
            hb = hs[p].astype(BF16)
            gmat, y0 = gy[p, c]
            y_scr[pl.ds(c * CHUNK, CHUNK), pl.ds(p * LANES, LANES)] = _dot(gmat, hb) + y0
            hs[p] = _dot(m[p, c], hb) + hadd[p, c]
    for p in range(pp):
        h_scr[first_pair + p] = hs[p]

    avg = jnp.where(blockdiag, 1.0 / RWKV_HEAD_DIM, 0.0).astype(BF16)
    for p in range(pp):
        cols = pl.ds(p * LANES, LANES)
        y = y_scr[:, cols]
        mean = _mm_rhs_exact(y, avg)
        yc = y - mean
        var = _mm_rhs_exact(yc * yc, avg)
        yn = yc * lax.rsqrt(var + LN_X_EPS)
        out = (yn * lnw_ref[:, cols] + lnb_ref[:, cols] + bv_ref[0, :, cols]) * g_ref[0, :, cols]
        o_ref[0, :, cols] = out.astype(BF16)


def _rwkv_scan(rp, am, bm, km, bh, kh, v, pc, bv, g, lnw, lnb):
    bsz, seq, d = rp.shape
    tb = 512
    npairs = d // LANES
    pp = 2
    width = pp * LANES
    tok = pl.BlockSpec((1, tb, width), lambda b, t, p: (b, t, p))
    vec = pl.BlockSpec((1, width), lambda b, t, p: (0, p))
    return pl.pallas_call(
        _rwkv_scan_kernel,
        grid=(bsz, seq // tb, npairs // pp),
        in_specs=[tok] * 7 + [pl.BlockSpec((1, tb // CHUNK, width), lambda b, t, p: (b, t, p)),
                              tok, tok, vec, vec],
        out_specs=tok,
        out_shape=jax.ShapeDtypeStruct((bsz, seq, d), BF16),
        scratch_shapes=[pltpu.VMEM((npairs, LANES, LANES), F32),
                        pltpu.VMEM((tb, width), F32)],
        compiler_params=_cparams("arbitrary", "arbitrary", "arbitrary"),
        name="rwkv_scan",
    )(rp, am, bm, km, bh, kh, v, pc, bv, g, lnw, lnb)


def _rope(x, cos, sin, lane):
    lo = QK_NOPE_DIM + QK_ROPE_DIM // 2
    partner = jnp.where(lane < lo,
                        -pltpu.roll(x, LANES - QK_ROPE_DIM // 2, axis=1),
                        pltpu.roll(x, QK_ROPE_DIM // 2, axis=1))
    return x * cos + partner * sin


def _mla_prep_kernel(z_ref, pos_ref, freq_ref, gq_ref, wuq_ref, gkv_ref, wukv_ref,
                     qn_ref, kn_ref, kshift_ref, q_ref, k_ref, vt_ref):
    z = z_ref[0]
    zq = z[:, :Q_LORA]
    zkv = z[:, Q_LORA:Q_LORA + KV_LORA]
    zpe = z[:, Q_LORA + KV_LORA:]
    ang = pos_ref[0] * freq_ref[...]
    cos = jnp.cos(ang)
    sin = jnp.sin(ang)
    lane = lax.broadcasted_iota(jnp.int32, (1, LANES), 1)
    nope = lane < QK_NOPE_DIM

    qa = zq * lax.rsqrt(jnp.mean(zq * zq, axis=-1, keepdims=True) + EPS) * gq_ref[...]
    q_all = _mm(qa, wuq_ref[...])
    kva = zkv * lax.rsqrt(jnp.mean(zkv * zkv, axis=-1, keepdims=True) + EPS) * gkv_ref[...]
    kv_all = _mm(kva, wukv_ref[...])
    pe = pltpu.roll(zpe, QK_NOPE_DIM, axis=1)
    scale = QK_HEAD_DIM ** -0.5 * LOG2_E
    for h in range(MLA_HEADS):
        q = q_all[:, h * HEAD_PAD:(h + 1) * HEAD_PAD]
        ms = jnp.sum(q * q, axis=-1, keepdims=True) * (1.0 / QK_HEAD_DIM)
        q = q * lax.rsqrt(ms + EPS) * qn_ref[...]
        q = _rope(q, cos, sin, lane) * scale
        q_ref[0, h] = jnp.where(lane == QK_HEAD_DIM, 1.0, q).astype(BF16)
        kv = kv_all[:, h * HEAD_PAD:(h + 1) * HEAD_PAD]
        k = jnp.where(nope, kv, pe)
        ms = jnp.sum(k * k, axis=-1, keepdims=True) * (1.0 / QK_HEAD_DIM)
        k = k * lax.rsqrt(ms + EPS) * kn_ref[...]
        k_ref[0, h] = (_rope(k, cos, sin, lane) + kshift_ref[...]).astype(BF16)
        vv = pltpu.roll(kv, LANES - QK_NOPE_DIM, axis=1)
        vv = jnp.where(nope, vv, jnp.where(lane == V_HEAD_DIM, 1.0, 0.0))
        vt_ref[0, h] = vv.T.astype(BF16)


def _mla_prep(zm3, pos, freq, gq, wuq, gkv, wukv, qn, kn, kshift):
    bsz, seq, cols = zm3.shape
    tm = 512
    tok = lambda b, j: (b, j, 0)
    head = lambda b, j: (b, 0, j, 0)
    return pl.pallas_call(
        _mla_prep_kernel,
        grid=(bsz, seq // tm),
        in_specs=[pl.BlockSpec((1, tm, cols), tok), pl.BlockSpec((1, tm, 1), tok),
                  _const_spec((1, LANES)), _const_spec((1, Q_LORA)), _const_spec(wuq.shape),
                  _const_spec((1, KV_LORA)), _const_spec(wukv.shape),
                  _const_spec((1, LANES)), _const_spec((1, LANES)), _const_spec((1, LANES))],
        out_specs=[pl.BlockSpec((1, MLA_HEADS, tm, HEAD_PAD), head),
                   pl.BlockSpec((1, MLA_HEADS, tm, HEAD_PAD), head),
                   pl.BlockSpec((1, MLA_HEADS, HEAD_PAD, tm), lambda b, j: (b, 0, 0, j))],
        out_shape=[jax.ShapeDtypeStruct((bsz, MLA_HEADS, seq, HEAD_PAD), BF16),
                   jax.ShapeDtypeStruct((bsz, MLA_HEADS, seq, HEAD_PAD), BF16),
                   jax.ShapeDtypeStruct((bsz, MLA_HEADS, HEAD_PAD, seq), BF16)],
        compiler_params=_cparams("arbitrary", "arbitrary"),
        name="mla_prep",
    )(zm3, pos, freq, gq, wuq, gkv, wukv, qn, kn, kshift)


def _attn_kernel(qi_ref, kj_ref, fixed_ref, q_ref, k_ref, vt_ref, o_ref, m_scr, acc_scr):
    p = pl.program_id(1)
    qi = qi_ref[p]
    kj = kj_ref[p]
    fixed_shift = fixed_ref[0] == 1
    bq = q_ref.shape[2]
    bk = k_ref.shape[2]
    heads = range(MLA_HEADS)

    @pl.when(kj == 0)
    def _():
        m_scr[...] = jnp.full(m_scr.shape, NEG_BIG, F32)
        acc_scr[...] = jnp.zeros(acc_scr.shape, F32)

    def scores(diagonal):
        st = [_dot_nt(k_ref[0, h], q_ref[0, h]) for h in heads]
        if diagonal:
            visible = (lax.broadcasted_iota(jnp.int32, (bk, bq), 0)
                       <= lax.broadcasted_iota(jnp.int32, (bk, bq), 1))
            st = [jnp.where(visible, x, NEG_BIG) for x in st]
        return st

    def block_fixed(diagonal):
        st = scores(diagonal)
        pr = [jnp.exp2(st[h]).astype(BF16) for h in heads]
        pv = [_dot(vt_ref[0, h], pr[h]) for h in heads]
        for h in heads:
            acc_scr[h] += pv[h]

    def block(diagonal):
        st = scores(diagonal)
        m_prev = [m_scr[h] for h in heads]
        m_new = [jnp.maximum(m_prev[h], jnp.max(st[h], axis=0, keepdims=True)) for h in heads]
        pr = [jnp.exp2(st[h] - m_new[h]).astype(BF16) for h in heads]
        alpha = [jnp.exp2(m_prev[h] - m_new[h]) for h in heads]
        pv = [_dot(vt_ref[0, h], pr[h]) for h in heads]
        for h in heads:
            acc_scr[h] = alpha[h] * acc_scr[h] + pv[h]
            m_scr[h] = m_new[h]

    for diagonal in (False, True):
        on_diag = (kj == qi) if diagonal else (kj != qi)
        pl.when(on_diag & fixed_shift)(functools.partial(block_fixed, diagonal))
        pl.when(on_diag & jnp.logical_not(fixed_shift))(functools.partial(block, diagonal))

    @pl.when(kj == qi)
    def _():
        for h in heads:
            acc = acc_scr[h]
            o = acc[:V_HEAD_DIM] * (1.0 / acc[V_HEAD_DIM:V_HEAD_DIM + 1])
            o_ref[0, h * V_HEAD_DIM:(h + 1) * V_HEAD_DIM, :] = o.astype(BF16)


def _attn(q, k, v, fixed):
    bsz, nh, seq, _ = q.shape
    bq = bk = 512
    nq = seq // bq
    qi = np.concatenate([np.full(i + 1, i) for i in range(nq)]).astype(np.int32)
    kj = np.concatenate([np.arange(i + 1) for i in range(nq)]).astype(np.int32)
    grid_spec = pltpu.PrefetchScalarGridSpec(
        num_scalar_prefetch=3,
        grid=(bsz, len(qi)),
        in_specs=[pl.BlockSpec((1, nh, bq, HEAD_PAD), lambda b, p, qi, kj, f: (b, 0, qi[p], 0)),
                  pl.BlockSpec((1, nh, bk, HEAD_PAD), lambda b, p, qi, kj, f: (b, 0, kj[p], 0)),
                  pl.BlockSpec((1, nh, HEAD_PAD, bk), lambda b, p, qi, kj, f: (b, 0, 0, kj[p]))],
        out_specs=pl.BlockSpec((1, nh * V_HEAD_DIM, bq), lambda b, p, qi, kj, f: (b, 0, qi[p])),
        scratch_shapes=[pltpu.VMEM((nh, 1, bq), F32), pltpu.VMEM((nh, HEAD_PAD, bq), F32)],
    )
    return pl.pallas_call(
        _attn_kernel,
        grid_spec=grid_spec,
        out_shape=jax.ShapeDtypeStruct((bsz, nh * V_HEAD_DIM, seq), BF16),
        compiler_params=_cparams("arbitrary", "arbitrary"),
        name="mla_attn",
    )(jnp.asarray(qi), jnp.asarray(kj), fixed, q, k, v)


def _first_index_of_max(vals, lane):
    mx = jnp.max(vals, axis=-1, keepdims=True)
    idx = jnp.min(jnp.where(vals == mx, lane, LANES), axis=-1, keepdims=True)
    return mx, idx


def _post_kernel(yr_ref, ymt_ref, sg_ref, x_ref, gt_ref, sh_ref, sc_ref, g_ref,
                 wb0_ref, wb1_ref, wo_ref, wr_ref, br_ref, x1_ref, h2_ref, cmb_ref, cnt_ref):
    sg = sg_ref[...].astype(F32)
    merged = (sg[:, :D_MODEL] * _dot(yr_ref[...], wb0_ref[...])
              + sg[:, D_MODEL:] * _dot_tn(ymt_ref[0], wb1_ref[...]))
    x1 = x_ref[...] + gt_ref[0] * _mm(merged, wo_ref[...])
    x1_ref[...] = x1
    ms = jnp.mean(x1 * x1, axis=-1, keepdims=True)
    h2 = x1 * lax.rsqrt(ms + EPS) * g_ref[...]
    h2 = h2 * (1.0 + sc_ref[0]) + sh_ref[0]
    h2_ref[...] = h2.astype(BF16)

    logits = _mm3(h2, wr_ref[...]) + br_ref[...]
    lane = lax.broadcasted_iota(jnp.int32, logits.shape, 1).astype(F32)
    gl = jnp.where((lane >= N_EXPERTS) & (lane < N_EXPERTS + N_GROUPS), logits, NEG_BIG)
    gmax, gidx = _first_index_of_max(gl, lane)
    p_g = 1.0 / jnp.sum(jnp.exp(gl - gmax), axis=-1, keepdims=True)
    first = (gidx - N_EXPERTS) * EXPERTS_PER_GROUP
    el = jnp.where((lane >= first) & (lane < first + EXPERTS_PER_GROUP), logits, NEG_BIG)
    m1, i1 = _first_index_of_max(el, lane)
    z = jnp.sum(jnp.exp(el - m1), axis=-1, keepdims=True)
    el2 = jnp.where(lane == i1, NEG_BIG, el)
    m2, i2 = _first_index_of_max(el2, lane)
    p1 = 1.0 / z
    p2 = jnp.exp(m2 - m1) / z
    tot = p1 + p2
    group_onehot = jnp.where(lane == gidx, 1.0, 0.0)
    cmb_ref[...] = (jnp.where(lane == i1, p1 / tot * p_g, 0.0)
                    + jnp.where(lane == i2, p2 / tot * p_g, 0.0) + group_onehot)
    cnt_ref[0] = jnp.broadcast_to(jnp.sum(group_onehot, axis=0, keepdims=True), cnt_ref.shape[1:])


def _post(yr, ym, sg, x2, gt, sh, sc, g, wb0, wb1, wo, wr, br, seq):
    n, d = x2.shape
    tm = 512
    per_b = seq // tm
    bmap = lambda i: (i // per_b, 0, 0)
    rowmap = lambda i: (i, 0)
    return pl.pallas_call(
        _post_kernel,
        grid=(n // tm,),
        in_specs=[pl.BlockSpec((tm, D_RWKV), rowmap),
                  pl.BlockSpec((1, D_MLA, tm), lambda i: (i // per_b, 0, i % per_b)),
                  pl.BlockSpec((tm, 2 * d), rowmap), pl.BlockSpec((tm, d), rowmap),
                  pl.BlockSpec((1, 1, d), bmap), pl.BlockSpec((1, 1, d), bmap),
                  pl.BlockSpec((1, 1, d), bmap), _const_spec((1, d)),
                  _const_spec(wb0.shape), _const_spec(wb1.shape), _const_spec(wo.shape),
                  _const_spec(wr.shape), _const_spec(br.shape)],
        out_specs=[pl.BlockSpec((tm, d), rowmap), pl.BlockSpec((tm, d), rowmap),
                   pl.BlockSpec((tm, LANES), rowmap),
                   pl.BlockSpec((1, 8, LANES), lambda i: (i, 0, 0))],
        out_shape=[jax.ShapeDtypeStruct((n, d), F32), jax.ShapeDtypeStruct((n, d), BF16),
                   jax.ShapeDtypeStruct((n, LANES), F32),
                   jax.ShapeDtypeStruct((n // tm, 8, LANES), F32)],
        compiler_params=_cparams("arbitrary"),
        name="post",
    )(yr, ym, sg, x2, gt, sh, sc, g, wb0, wb1, wo, wr, br)


MOE_TILE = 1024
MOE_CHUNK = 288


def _moe_kernel(cnt_ref, h_ref, cmb_ref, x1_ref, gt_ref, wg_ref, wu_ref, wd_ref, o_ref,
                key_scr):
    i = pl.program_id(0)
    g = pl.program_id(1)
    tm = h_ref.shape[0]
    lane = lax.broadcasted_iota(jnp.int32, (1, LANES), 1)

    @pl.when(g == 0)
    def _():
        o_ref[...] = x1_ref[...]
        onehot = jnp.where((lane >= N_EXPERTS) & (lane < N_EXPERTS + N_GROUPS), cmb_ref[...], 0.0)
        before = (lax.broadcasted_iota(jnp.int32, (tm, tm), 1)
                  < lax.broadcasted_iota(jnp.int32, (tm, tm), 0))
        rank = _dot(jnp.where(before, 1.0, 0.0).astype(BF16), onehot.astype(BF16))
        key = jnp.where(onehot > 0.5, rank, -1.0)
        key_scr[...] = key.T

    key_row = key_scr[pl.ds(N_EXPERTS + g, 1), :]
    h = h_ref[...]
    cmb = cmb_ref[...]
    cmb_hi, cmb_lo = _split(cmb)
    gt = gt_ref[0]
    first_lane = g * EXPERTS_PER_GROUP
    n_chunks = (cnt_ref[i * N_GROUPS + g] + (MOE_CHUNK - 1)) // MOE_CHUNK

    def chunk(c, carry):
        slot = (c * MOE_CHUNK + lax.broadcasted_iota(jnp.int32, (MOE_CHUNK, 1), 0)).astype(F32)
        sel = jnp.where(key_row == slot, 1.0, 0.0).astype(BF16)
        xg = _dot(sel, h).astype(BF16)
        cw = _dot(sel, cmb_hi) + _dot(sel, cmb_lo)
        y = jnp.zeros((MOE_CHUNK, o_ref.shape[1]), F32)
        for e in range(EXPERTS_PER_GROUP):
            gate = _dot(xg, wg_ref[e])
            hid = gate * _sigmoid(gate) * _dot(xg, wu_ref[e])
            cwe = jnp.sum(jnp.where(lane == first_lane + e, cw, 0.0), axis=-1, keepdims=True)
            y = y + _mm(hid * cwe, wd_ref[e])
        o_ref[...] += _dot_tn(sel, (y * gt).astype(BF16))
        return carry

    lax.fori_loop(0, n_chunks, chunk, 0)


def _moe(cnt, h2, cmb, x1, gt, wg, wu, wd, seq):
    n, d = x1.shape
    tm = MOE_TILE
    per_b = seq // tm
    rowmap = lambda i, g, cnt: (i, 0)
    wmap = lambda i, g, cnt: (g, 0, 0)
    grid_spec = pltpu.PrefetchScalarGridSpec(
        num_scalar_prefetch=1,
        grid=(n // tm, N_GROUPS),
        in_specs=[pl.BlockSpec((tm, d), rowmap), pl.BlockSpec((tm, LANES), rowmap),
                  pl.BlockSpec((tm, d), rowmap),
                  pl.BlockSpec((1, 1, d), lambda i, g, cnt: (i // per_b, 0, 0)),
                  pl.BlockSpec((EXPERTS_PER_GROUP, d, D_EXPERT), wmap),
                  pl.BlockSpec((EXPERTS_PER_GROUP, d, D_EXPERT), wmap),
                  pl.BlockSpec((EXPERTS_PER_GROUP, D_EXPERT, d), wmap)],
        out_specs=pl.BlockSpec((tm, d), rowmap),
        scratch_shapes=[pltpu.VMEM((LANES, tm), F32)],
    )
    return pl.pallas_call(
        _moe_kernel,
        grid_spec=grid_spec,
        out_shape=jax.ShapeDtypeStruct((n, d), F32),
        compiler_params=_cparams("arbitrary", "arbitrary"),
        name="moe",
    )(cnt, h2, cmb, x1, gt, wg, wu, wd)


def _pad_cols(w, n):
    return jnp.pad(w, ((0, 0), (0, n - w.shape[1])))


def kernel(x, c, positions, w_ada, b_ada, g_norm_mix, w_in, mu_shift, w0, w_decay_up, a0, w_a_up, w_g_up, k_k, k_a, r_k, ln_x_w, ln_x_b, g_q_a, w_uq, g_kv_a, w_ukv, q_norm, k_norm, w_branch, w_out, g_norm_ffn, w_router_group, b_router_group, w_router_expert, b_router_expert, w_e_gate, w_e_up, w_e_down):
    bsz, seq, d = x.shape
    n = bsz * seq
    row = lambda a: a.reshape(1, -1)

    freqs = ROPE_THETA ** (-(jnp.arange(0, QK_ROPE_DIM, 2, dtype=F32) / QK_ROPE_DIM))
    freq = jnp.zeros((1, LANES), F32).at[0, QK_NOPE_DIM:QK_NOPE_DIM + QK_ROPE_DIM].set(
        jnp.concatenate([freqs, freqs]))
    pos = positions.astype(F32).reshape(bsz, seq, 1)

    x2 = x.reshape(n, d)
    for l in range(w_ada.shape[0]):
        mod = _ada(c, w_ada[l], b_ada[l])
        sh_m, sc_m, gt_m, sh_f, sc_f, gt_f = [m.reshape(bsz, 1, d) for m in jnp.split(mod, 6, axis=-1)]

        w_in_l = w_in[l].astype(BF16)
        wr = w_in_l[:, :RWKV_COLS]
        wm = _pad_cols(w_in_l[:, RWKV_COLS:RWKV_COLS + MLA_COLS], MLA_COLS_PAD)
        wg = w_in_l[:, RWKV_COLS + MLA_COLS:]
        zr, zm, sg = _inproj(x2, sh_m, sc_m, row(g_norm_mix[l]), wr, wm, wg, seq)

        zeros_lora = jnp.zeros((DECAY_LORA, D_RWKV), F32)
        wdec = jnp.concatenate([w_decay_up[l], zeros_lora], axis=0)
        wa = jnp.concatenate([zeros_lora, w_a_up[l]], axis=0)
        prep = _rwkv_prep(zr.reshape(bsz, seq, RWKV_COLS), row(mu_shift[l]), row(w0[l]), wdec,
                          row(a0[l]), wa, w_g_up[l], row(k_k[l]), row(k_a[l]), row(r_k[l]))
        y_rwkv = _rwkv_scan(*prep, row(ln_x_w[l]), row(ln_x_b[l]))

        wuq = w_uq[l].reshape(Q_LORA, MLA_HEADS, QK_HEAD_DIM)
        wuq = jnp.pad(wuq, ((0, 0), (0, 0), (0, HEAD_PAD - QK_HEAD_DIM))).reshape(Q_LORA, -1)
        qn = _pad_cols(row(q_norm[l]), LANES)
        kn = _pad_cols(row(k_norm[l]), LANES)
        bound = (QK_HEAD_DIM * jnp.max(jnp.abs(q_norm[l])) * jnp.max(jnp.abs(k_norm[l]))
                 * QK_HEAD_DIM ** -0.5 * LOG2_E)
        fixed = 2.0 * bound <= MAX_FIXED_SHIFT_SPAN
        kshift = jnp.zeros((1, LANES), F32).at[0, QK_HEAD_DIM].set(jnp.where(fixed, -bound, 0.0))
        q, k, v = _mla_prep(zm.reshape(bsz, seq, MLA_COLS_PAD), pos, freq, row(g_q_a[l]),
                            wuq.astype(BF16), row(g_kv_a[l]), w_ukv[l].astype(BF16), qn, kn, kshift)
        y_mla = _attn(q, k, v, fixed.astype(jnp.int32).reshape(1))

        w_router = _pad_cols(jnp.concatenate([w_router_expert[l], w_router_group[l]], axis=1), LANES)
        b_router = _pad_cols(row(jnp.concatenate([b_router_expert[l], b_router_group[l]])), LANES)
        x1, h2, cmb, cnt = _post(y_rwkv.reshape(n, D_RWKV), y_mla, sg, x2,
                                 gt_m, sh_f, sc_f, row(g_norm_ffn[l]),
                                 w_branch[l, 0].astype(BF16), w_branch[l, 1].astype(BF16),
                                 w_out[l].astype(BF16), w_router, b_router, seq)

        cnt = cnt[:, 0, N_EXPERTS:N_EXPERTS + N_GROUPS].reshape(n // MOE_TILE, -1, N_GROUPS)
        cnt = jnp.sum(cnt, axis=1).astype(jnp.int32).reshape(-1)
        x2 = _moe(cnt, h2, cmb, x1, gt_f, w_e_gate[l].astype(BF16), w_e_up[l].astype(BF16),
                  w_e_down[l].astype(BF16), seq)
    return x2.reshape(bsz, seq, d)
```

```python
import functools

import numpy as np
import jax
import jax.numpy as jnp
from jax import lax
from jax.experimental import pallas as pl
from jax.experimental.pallas import tpu as pltpu

F32 = jnp.float32
BF16 = jnp.bfloat16

D_MODEL = 1024
EPS = 1e-6
RWKV_HEADS = 8
RWKV_HEAD_DIM = 64
D_RWKV = 512
DECAY_LORA = 64
AAA_LORA = 64
GATE_LORA = 128
LN_X_EPS = 64e-5
MLA_HEADS = 8
QK_NOPE_DIM = 64
QK_ROPE_DIM = 32
QK_HEAD_DIM = 96
V_HEAD_DIM = 64
D_MLA = 512
Q_LORA = 256
KV_LORA = 128
ROPE_THETA = 10000.0
RWKV_COLS = 1792
MLA_COLS = 416
MLA_COLS_PAD = 512
N_GROUPS = 4
EXPERTS_PER_GROUP = 8
N_EXPERTS = 32
D_EXPERT = 256

LANES = 128
CHUNK = 64
HEAD_PAD = 128
NEG_BIG = -1e30
LOG2_E = 1.4426950408889634
MAX_FIXED_SHIFT_SPAN = 80.0
VMEM_LIMIT = 56 * 1024 * 1024


def _cparams(*sem):
    return pltpu.CompilerParams(dimension_semantics=sem, vmem_limit_bytes=VMEM_LIMIT)


def _dot(a, b):
    return jnp.dot(a, b, preferred_element_type=F32)


def _dot_nt(a, b):
    return lax.dot_general(a, b, (((1,), (1,)), ((), ())), preferred_element_type=F32)


def _dot_tn(a, b):
    return lax.dot_general(a, b, (((0,), (0,)), ((), ())), preferred_element_type=F32)


def _mm(a, b):
    return _dot(a.astype(BF16), b.astype(BF16))


def _split(a):
    hi = a.astype(BF16)
    lo = (a - hi.astype(F32)).astype(BF16)
    return hi, lo


def _mm_rhs_exact(a, b):
    hi, lo = _split(a)
    return _dot(hi, b) + _dot(lo, b)


def _mm_lhs_exact(a, b):
    hi, lo = _split(b)
    return _dot(a, hi) + _dot(a, lo)


def _mm3(a, b):
    ah, al = _split(a)
    bh, bl = _split(b)
    return _dot(ah, bh) + (_dot(ah, bl) + _dot(al, bh))


def _sigmoid(x):
    return 1.0 / (1.0 + jnp.exp(-x))


def _const_spec(shape):
    nd = len(shape)
    return pl.BlockSpec(shape, lambda *_: (0,) * nd)


def _ada_kernel(c_ref, w_ref, b_ref, o_ref):
    c = c_ref[...]
    s = c * _sigmoid(c)
    o_ref[...] = _mm3(s, w_ref[...]) + b_ref[...]


def _ada(c, w, b):
    bsz, d = c.shape
    n = w.shape[1]
    tn = 1024
    return pl.pallas_call(
        _ada_kernel,
        grid=(n // tn,),
        in_specs=[pl.BlockSpec((bsz, d), lambda j: (0, 0)),
                  pl.BlockSpec((d, tn), lambda j: (0, j)),
                  pl.BlockSpec((1, tn), lambda j: (0, j))],
        out_specs=pl.BlockSpec((bsz, tn), lambda j: (0, j)),
        out_shape=jax.ShapeDtypeStruct((bsz, n), F32),
        compiler_params=_cparams("arbitrary"),
        name="ada",
    )(c, w, b.reshape(1, n))


def _inproj_kernel(x_ref, sh_ref, sc_ref, g_ref, wr_ref, wm_ref, wg_ref,
                   zr_ref, zm_ref, sg_ref):
    x = x_ref[...]
    ms = jnp.mean(x * x, axis=-1, keepdims=True)
    h = x * lax.rsqrt(ms + EPS) * g_ref[...]
    h = h * (1.0 + sc_ref[0]) + sh_ref[0]
    hb = h.astype(BF16)
    zr_ref[...] = _dot(hb, wr_ref[...])
    zm_ref[...] = _dot(hb, wm_ref[...])
    sg_ref[...] = _sigmoid(_dot(hb, wg_ref[...])).astype(BF16)


def _inproj(x2, sh, sc, g, wr, wm, wg, seq):
    n, d = x2.shape
    tm = 512
    per_b = seq // tm
    bmap = lambda i: (i // per_b, 0, 0)
    return pl.pallas_call(
        _inproj_kernel,
        grid=(n // tm,),
        in_specs=[pl.BlockSpec((tm, d), lambda i: (i, 0)),
                  pl.BlockSpec((1, 1, d), bmap),
                  pl.BlockSpec((1, 1, d), bmap),
                  _const_spec((1, d)),
                  _const_spec(wr.shape), _const_spec(wm.shape), _const_spec(wg.shape)],
        out_specs=[pl.BlockSpec((tm, wr.shape[1]), lambda i: (i, 0)),
                   pl.BlockSpec((tm, wm.shape[1]), lambda i: (i, 0)),
                   pl.BlockSpec((tm, wg.shape[1]), lambda i: (i, 0))],
        out_shape=[jax.ShapeDtypeStruct((n, wr.shape[1]), F32),
                   jax.ShapeDtypeStruct((n, wm.shape[1]), F32),
                   jax.ShapeDtypeStruct((n, wg.shape[1]), BF16)],
        compiler_params=_cparams("arbitrary"),
        name="inproj",
    )(x2, sh, sc, g, wr, wm, wg)


def _rwkv_prep_kernel(z_ref, prev_ref, mu_ref, w0_ref, wdec_ref, a0_ref, wa_ref, wgu_ref,
                      kk_ref, ka_ref, rk_ref, ones_ref, tri_ref, blk_ref,
                      rp_ref, am_ref, bm_ref, km_ref, bh_ref, kh_ref, v_ref,
                      pc_ref, bv_ref, g_ref):
    tt = z_ref.shape[1]
    z = z_ref[0]
    prev = prev_ref[0][7:8, :]
    prev = jnp.where(pl.program_id(1) == 0, 0.0, prev)
    row = lax.broadcasted_iota(jnp.int32, (tt, 1), 0)
    zs = jnp.where(row == 0, prev, pltpu.roll(z, 1, axis=0))
    z = z + (zs - z) * mu_ref[...]
    zr = z[:, 0:D_RWKV]
    zk = z[:, D_RWKV:2 * D_RWKV]
    zv = z[:, 2 * D_RWKV:3 * D_RWKV]
    zwa = z[:, 3 * D_RWKV:3 * D_RWKV + DECAY_LORA + AAA_LORA]
    zg = z[:, 3 * D_RWKV + DECAY_LORA + AAA_LORA:]

    u = -(w0_ref[...] + _mm3(jnp.tanh(zwa), wdec_ref[...]))
    softplus = jnp.maximum(u, 0.0) + jnp.log(1.0 + jnp.exp(-jnp.abs(u)))
    logw = -jnp.exp(-softplus - 0.5)
    a = _sigmoid(a0_ref[...] + _mm(zwa, wa_ref[...]))
    g = _mm(_sigmoid(zg), wgu_ref[...])

    ones_blk = ones_ref[...]
    xk = zk * kk_ref[...]
    ss = _mm_rhs_exact(xk * xk, ones_blk)
    kk = xk * lax.rsqrt(jnp.maximum(ss, 1e-24))
    k = zk * (1.0 + (a - 1.0) * ka_ref[...])
    b = kk * a
    bonus = _mm_rhs_exact(zr * k * rk_ref[...], ones_blk)

    cum = _mm_lhs_exact(tri_ref[...], logw)
    tot = _mm_lhs_exact(blk_ref[...], logw)
    e_neg = jnp.exp(-cum)
    e_tot = jnp.exp(tot)
    bm = b * e_neg
    km = k * e_neg
    rp_ref[0] = (zr * jnp.exp(cum)).astype(BF16)
    am_ref[0] = (-kk * jnp.exp(cum - logw)).astype(BF16)
    bm_ref[0] = bm.astype(BF16)
    km_ref[0] = km.astype(BF16)
    bh_ref[0] = (bm * e_tot).astype(BF16)
    kh_ref[0] = (km * e_tot).astype(BF16)
    v_ref[0] = zv.astype(BF16)
    bv_ref[0] = bonus * zv
    g_ref[0] = g
    nc = tt // CHUNK
    pick = (lax.broadcasted_iota(jnp.int32, (nc, tt), 1)
            == CHUNK * lax.broadcasted_iota(jnp.int32, (nc, tt), 0))
    pc_ref[0] = _mm_lhs_exact(jnp.where(pick, 1.0, 0.0).astype(BF16), e_tot)


def _rwkv_prep(zr3, mu, w0, wdec, a0, wa, wgu, k_k, k_a, r_k):
    bsz, seq, cols = zr3.shape
    tt = 512
    d = D_RWKV
    ones_blk = jnp.asarray(np.kron(np.eye(RWKV_HEADS), np.ones((RWKV_HEAD_DIM, RWKV_HEAD_DIM))), BF16)
    cidx = np.arange(tt) // CHUNK
    same = cidx[:, None] == cidx[None, :]
    tri = jnp.asarray(same & (np.arange(tt)[None, :] <= np.arange(tt)[:, None]), BF16)
    blk = jnp.asarray(same, BF16)
    tok = lambda b, j: (b, j, 0)
    big = pl.BlockSpec((1, tt, d), tok)
    out_bf = jax.ShapeDtypeStruct((bsz, seq, d), BF16)
    out_f = jax.ShapeDtypeStruct((bsz, seq, d), F32)
    return pl.pallas_call(
        _rwkv_prep_kernel,
        grid=(bsz, seq // tt),
        in_specs=[pl.BlockSpec((1, tt, cols), tok),
                  pl.BlockSpec((1, 8, cols), lambda b, j: (b, jnp.maximum(j * (tt // 8) - 1, 0), 0)),
                  _const_spec((1, cols)), _const_spec((1, d)), _const_spec(wdec.shape),
                  _const_spec((1, d)), _const_spec(wa.shape), _const_spec(wgu.shape),
                  _const_spec((1, d)), _const_spec((1, d)), _const_spec((1, d)),
                  _const_spec((d, d)), _const_spec((tt, tt)), _const_spec((tt, tt))],
        out_specs=[big] * 7 + [pl.BlockSpec((1, tt // CHUNK, d), tok), big, big],
        out_shape=[out_bf] * 7 + [jax.ShapeDtypeStruct((bsz, seq // CHUNK, d), F32), out_f, out_f],
        compiler_params=_cparams("arbitrary", "arbitrary"),
        name="rwkv_prep",
    )(zr3, zr3, mu, w0, wdec, a0, wa, wgu, k_k, k_a, r_k, ones_blk, tri, blk)


def _rwkv_scan_kernel(rp_ref, am_ref, bm_ref, km_ref, bh_ref, kh_ref, v_ref, pc_ref,
                      bv_ref, g_ref, lnw_ref, lnb_ref, o_ref, h_scr, y_scr):
    tb = rp_ref.shape[1]
    nc = tb // CHUNK
    pp = rp_ref.shape[2] // LANES
    first_pair = pl.program_id(2) * pp

    @pl.when(pl.program_id(1) == 0)
    def _():
        for p in range(pp):
            h_scr[first_pair + p] = jnp.zeros((LANES, LANES), F32)

    lane = lax.broadcasted_iota(jnp.int32, (1, LANES), 1)
    lane2 = lax.broadcasted_iota(jnp.int32, (1, 2 * LANES), 1)
    in_head = [lane < RWKV_HEAD_DIM, lane >= RWKV_HEAD_DIM]
    in_head2 = [(lane2 % LANES) < RWKV_HEAD_DIM, (lane2 % LANES) >= RWKV_HEAD_DIM]

    def blk(x):
        masks = in_head if x.shape[1] == LANES else in_head2
        zero = jnp.zeros((), x.dtype)
        return jnp.concatenate([jnp.where(masks[0], x, zero), jnp.where(masks[1], x, zero)], axis=0)

    rr = lax.broadcasted_iota(jnp.int32, (CHUNK, LANES), 0)
    cc = lax.broadcasted_iota(jnp.int32, (CHUNK, LANES), 1) % CHUNK
    eye_packed = cc == rr
    r2 = lax.broadcasted_iota(jnp.int32, (2 * CHUNK, LANES), 0)
    c2 = lax.broadcasted_iota(jnp.int32, (2 * CHUNK, LANES), 1) % CHUNK
    causal2 = ((r2 < CHUNK) & (c2 < r2)) | ((r2 >= CHUNK) & (c2 <= r2 - CHUNK))
    level_masks = [((rr >> (k + 1)) == (cc >> (k + 1))) & ((rr >> k) == (cc >> k) + 1)
                   for k in range(6)]
    r128 = lax.broadcasted_iota(jnp.int32, (LANES, LANES), 0)
    c128 = lax.broadcasted_iota(jnp.int32, (LANES, LANES), 1)
    blockdiag = (r128 < RWKV_HEAD_DIM) == (c128 < RWKV_HEAD_DIM)
    eye = r128 == c128

    slabs = [(p, c) for p in range(pp) for c in range(nc)]

    def load(ref, p, c):
        return ref[0, pl.ds(c * CHUNK, CHUNK), pl.ds(p * LANES, LANES)]

    rp = {s: load(rp_ref, *s) for s in slabs}
    am = {s: load(am_ref, *s) for s in slabs}
    v = {s: load(v_ref, *s) for s in slabs}

    a_b, a_k = {}, {}
    for s in slabs:
        lhs = jnp.concatenate([am[s], rp[s]], axis=0)
        a_b[s] = jnp.where(causal2, _dot_nt(lhs, blk(load(bm_ref, *s))), 0.0)
        a_k[s] = jnp.where(causal2, _dot_nt(lhs, blk(load(km_ref, *s))), 0.0)
    akv = {s: _dot(a_k[s].astype(BF16), blk(v[s])) for s in slabs}

    a_ab = {s: a_b[s][:CHUNK] for s in slabs}
    tinv = {s: jnp.where(eye_packed, 1.0, 0.0) + jnp.where(level_masks[0], a_ab[s], 0.0)
            for s in slabs}
    for lm in level_masks[1:]:
        tbf = {s: tinv[s].astype(BF16) for s in slabs}
        inner = {s: _dot(jnp.where(lm, a_ab[s], 0.0).astype(BF16), blk(tbf[s])) for s in slabs}
        tinv = {s: tinv[s] + _dot(tbf[s], blk(inner[s].astype(BF16))) for s in slabs}
    z = {s: _dot(tinv[s].astype(BF16),
                 blk(jnp.concatenate([am[s], akv[s][:CHUNK].astype(BF16)], axis=1)))
         for s in slabs}
    g2 = {s: _dot(a_b[s][CHUNK:].astype(BF16), blk(z[s].astype(BF16))) for s in slabs}

    gy, m, hadd = {}, {}, {}
    for s in slabs:
        p, c = s
        mh = _dot_tn(load(bh_ref, p, c), z[s].astype(BF16))
        kv = _dot_tn(load(kh_ref, p, c), v[s])
        pc = pc_ref[0, c:c + 1, pl.ds(p * LANES, LANES)]
        m[s] = (jnp.where(blockdiag, mh[:, :LANES], 0.0) + jnp.where(eye, pc, 0.0)).astype(BF16)
        hadd[s] = jnp.where(blockdiag, mh[:, LANES:] + kv, 0.0)
        gy[s] = ((rp[s].astype(F32) + g2[s][:, :LANES]).astype(BF16),
                 g2[s][:, LANES:] + akv[s][CHUNK:])

    hs = [h_scr[first_pair + p] for p in range(pp)]
    for c in range(nc):
        for p in range(pp):
            hb = hs[p].astype(BF16)
            gmat, y0 = gy[p, c]
            y_scr[pl.ds(c * CHUNK, CHUNK), pl.ds(p * LANES, LANES)] = _dot(gmat, hb) + y0
            hs[p] = _dot(m[p, c], hb) + hadd[p, c]
    for p in range(pp):
        h_scr[first_pair + p] = hs[p]

    avg = jnp.where(blockdiag, 1.0 / RWKV_HEAD_DIM, 0.0).astype(BF16)
    for p in range(pp):
        cols = pl.ds(p * LANES, LANES)
        y = y_scr[:, cols]
        mean = _mm_rhs_exact(y, avg)
        yc = y - mean
        var = _mm_rhs_exact(yc * yc, avg)
        yn = yc * lax.rsqrt(var + LN_X_EPS)
        out = (yn * lnw_ref[:, cols] + lnb_ref[:, cols] + bv_ref[0, :, cols]) * g_ref[0, :, cols]
        o_ref[0, :, cols] = out.astype(BF16)


def _rwkv_scan(rp, am, bm, km, bh, kh, v, pc, bv, g, lnw, lnb):
    bsz, seq, d = rp.shape
    tb = 512
    npairs = d // LANES
    pp = 4
    width = pp * LANES
    tok = pl.BlockSpec((1, tb, width), lambda b, t, p: (b, t, p))
    vec = pl.BlockSpec((1, width), lambda b, t, p: (0, p))
    return pl.pallas_call(
        _rwkv_scan_kernel,
        grid=(bsz, seq // tb, npairs // pp),
        in_specs=[tok] * 7 + [pl.BlockSpec((1, tb // CHUNK, width), lambda b, t, p: (b, t, p)),
                              tok, tok, vec, vec],
        out_specs=tok,
        out_shape=jax.ShapeDtypeStruct((bsz, seq, d), BF16),
        scratch_shapes=[pltpu.VMEM((npairs, LANES, LANES), F32),
                        pltpu.VMEM((tb, width), F32)],
        compiler_params=_cparams("arbitrary", "arbitrary", "arbitrary"),
        name="rwkv_scan",
    )(rp, am, bm, km, bh, kh, v, pc, bv, g, lnw, lnb)


def _rope(x, cos, sin, lane):
    lo = QK_NOPE_DIM + QK_ROPE_DIM // 2
    partner = jnp.where(lane < lo,
                        -pltpu.roll(x, LANES - QK_ROPE_DIM // 2, axis=1),
                        pltpu.roll(x, QK_ROPE_DIM // 2, axis=1))
    return x * cos + partner * sin


def _mla_prep_kernel(z_ref, pos_ref, freq_ref, gq_ref, wuq_ref, gkv_ref, wukv_ref,
                     qn_ref, kn_ref, kshift_ref, q_ref, k_ref, vt_ref):
    z = z_ref[0]
    zq = z[:, :Q_LORA]
    zkv = z[:, Q_LORA:Q_LORA + KV_LORA]
    zpe = z[:, Q_LORA + KV_LORA:]
    ang = pos_ref[0] * freq_ref[...]
    cos = jnp.cos(ang)
    sin = jnp.sin(ang)
    lane = lax.broadcasted_iota(jnp.int32, (1, LANES), 1)
    nope = lane < QK_NOPE_DIM

    qa = zq * lax.rsqrt(jnp.mean(zq * zq, axis=-1, keepdims=True) + EPS) * gq_ref[...]
    q_all = _mm(qa, wuq_ref[...])
    kva = zkv * lax.rsqrt(jnp.mean(zkv * zkv, axis=-1, keepdims=True) + EPS) * gkv_ref[...]
    kv_all = _mm(kva, wukv_ref[...])
    pe = pltpu.roll(zpe, QK_NOPE_DIM, axis=1)
    scale = QK_HEAD_DIM ** -0.5 * LOG2_E
    for h in range(MLA_HEADS):
        q = q_all[:, h * HEAD_PAD:(h + 1) * HEAD_PAD]
        ms = jnp.sum(q * q, axis=-1, keepdims=True) * (1.0 / QK_HEAD_DIM)
        q = q * lax.rsqrt(ms + EPS) * qn_ref[...]
        q = _rope(q, cos, sin, lane) * scale
        q_ref[0, h] = jnp.where(lane == QK_HEAD_DIM, 1.0, q).astype(BF16)
        kv = kv_all[:, h * HEAD_PAD:(h + 1) * HEAD_PAD]
        k = jnp.where(nope, kv, pe)
        ms = jnp.sum(k * k, axis=-1, keepdims=True) * (1.0 / QK_HEAD_DIM)
        k = k * lax.rsqrt(ms + EPS) * kn_ref[...]
        k_ref[0, h] = (_rope(k, cos, sin, lane) + kshift_ref[...]).astype(BF16)
        vv = pltpu.roll(kv, LANES - QK_NOPE_DIM, axis=1)
        vv = jnp.where(nope, vv, jnp.where(lane == V_HEAD_DIM, 1.0, 0.0))
        vt_ref[0, h] = vv.T.astype(BF16)


def _mla_prep(zm3, pos, freq, gq, wuq, gkv, wukv, qn, kn, kshift):
    bsz, seq, cols = zm3.shape
    tm = 512
    tok = lambda b, j: (b, j, 0)
    head = lambda b, j: (b, 0, j, 0)
    return pl.pallas_call(
        _mla_prep_kernel,
        grid=(bsz, seq // tm),
        in_specs=[pl.BlockSpec((1, tm, cols), tok), pl.BlockSpec((1, tm, 1), tok),
                  _const_spec((1, LANES)), _const_spec((1, Q_LORA)), _const_spec(wuq.shape),
                  _const_spec((1, KV_LORA)), _const_spec(wukv.shape),
                  _const_spec((1, LANES)), _const_spec((1, LANES)), _const_spec((1, LANES))],
        out_specs=[pl.BlockSpec((1, MLA_HEADS, tm, HEAD_PAD), head),
                   pl.BlockSpec((1, MLA_HEADS, tm, HEAD_PAD), head),
                   pl.BlockSpec((1, MLA_HEADS, HEAD_PAD, tm), lambda b, j: (b, 0, 0, j))],
        out_shape=[jax.ShapeDtypeStruct((bsz, MLA_HEADS, seq, HEAD_PAD), BF16),
                   jax.ShapeDtypeStruct((bsz, MLA_HEADS, seq, HEAD_PAD), BF16),
                   jax.ShapeDtypeStruct((bsz, MLA_HEADS, HEAD_PAD, seq), BF16)],
        compiler_params=_cparams("arbitrary", "arbitrary"),
        name="mla_prep",
    )(zm3, pos, freq, gq, wuq, gkv, wukv, qn, kn, kshift)


def _attn_kernel(qi_ref, kj_ref, fixed_ref, q_ref, k_ref, vt_ref, o_ref, m_scr, acc_scr):
    p = pl.program_id(1)
    qi = qi_ref[p]
    kj = kj_ref[p]
    fixed_shift = fixed_ref[0] == 1
    bq = q_ref.shape[2]
    bk = k_ref.shape[2]
    heads = range(MLA_HEADS)

    @pl.when(kj == 0)
    def _():
        m_scr[...] = jnp.full(m_scr.shape, NEG_BIG, F32)
        acc_scr[...] = jnp.zeros(acc_scr.shape, F32)

    def scores(diagonal):
        st = [_dot_nt(k_ref[0, h], q_ref[0, h]) for h in heads]
        if diagonal:
            visible = (lax.broadcasted_iota(jnp.int32, (bk, bq), 0)
                       <= lax.broadcasted_iota(jnp.int32, (bk, bq), 1))
            st = [jnp.where(visible, x, NEG_BIG) for x in st]
        return st

    def block_fixed(diagonal):
        st = scores(diagonal)
        pr = [jnp.exp2(st[h]).astype(BF16) for h in heads]
        pv = [_dot(vt_ref[0, h], pr[h]) for h in heads]
        for h in heads:
            acc_scr[h] += pv[h]

    def block(diagonal):
        st = scores(diagonal)
        m_prev = [m_scr[h] for h in heads]
        m_new = [jnp.maximum(m_prev[h], jnp.max(st[h], axis=0, keepdims=True)) for h in heads]
        pr = [jnp.exp2(st[h] - m_new[h]).astype(BF16) for h in heads]
        alpha = [jnp.exp2(m_prev[h] - m_new[h]) for h in heads]
        pv = [_dot(vt_ref[0, h], pr[h]) for h in heads]
        for h in heads:
            acc_scr[h] = alpha[h] * acc_scr[h] + pv[h]
            m_scr[h] = m_new[h]

    for diagonal in (False, True):
        on_diag = (kj == qi) if diagonal else (kj != qi)
        pl.when(on_diag & fixed_shift)(functools.partial(block_fixed, diagonal))
        pl.when(on_diag & jnp.logical_not(fixed_shift))(functools.partial(block, diagonal))

    @pl.when(kj == qi)
    def _():
        for h in heads:
            acc = acc_scr[h]
            o = acc[:V_HEAD_DIM] * (1.0 / acc[V_HEAD_DIM:V_HEAD_DIM + 1])
            o_ref[0, h * V_HEAD_DIM:(h + 1) * V_HEAD_DIM, :] = o.astype(BF16)


def _attn(q, k, v, fixed):
    bsz, nh, seq, _ = q.shape
    bq = bk = 512
    nq = seq // bq
    qi = np.concatenate([np.full(i + 1, i) for i in range(nq)]).astype(np.int32)
    kj = np.concatenate([np.arange(i + 1) for i in range(nq)]).astype(np.int32)
    grid_spec = pltpu.PrefetchScalarGridSpec(
        num_scalar_prefetch=3,
        grid=(bsz, len(qi)),
        in_specs=[pl.BlockSpec((1, nh, bq, HEAD_PAD), lambda b, p, qi, kj, f: (b, 0, qi[p], 0)),
                  pl.BlockSpec((1, nh, bk, HEAD_PAD), lambda b, p, qi, kj, f: (b, 0, kj[p], 0)),
                  pl.BlockSpec((1, nh, HEAD_PAD, bk), lambda b, p, qi, kj, f: (b, 0, 0, kj[p]))],
        out_specs=pl.BlockSpec((1, nh * V_HEAD_DIM, bq), lambda b, p, qi, kj, f: (b, 0, qi[p])),
        scratch_shapes=[pltpu.VMEM((nh, 1, bq), F32), pltpu.VMEM((nh, HEAD_PAD, bq), F32)],
    )
    return pl.pallas_call(
        _attn_kernel,
        grid_spec=grid_spec,
        out_shape=jax.ShapeDtypeStruct((bsz, nh * V_HEAD_DIM, seq), BF16),
        compiler_params=_cparams("arbitrary", "arbitrary"),
        name="mla_attn",
    )(jnp.asarray(qi), jnp.asarray(kj), fixed, q, k, v)


def _first_index_of_max(vals, lane):
    mx = jnp.max(vals, axis=-1, keepdims=True)
    idx = jnp.min(jnp.where(vals == mx, lane, LANES), axis=-1, keepdims=True)
    return mx, idx


def _post_kernel(yr_ref, ymt_ref, sg_ref, x_ref, gt_ref, sh_ref, sc_ref, g_ref,
                 wb0_ref, wb1_ref, wo_ref, wr_ref, br_ref, x1_ref, h2_ref, cmb_ref, cnt_ref):
    sg = sg_ref[...].astype(F32)
    merged = (sg[:, :D_MODEL] * _dot(yr_ref[...], wb0_ref[...])
              + sg[:, D_MODEL:] * _dot_tn(ymt_ref[0], wb1_ref[...]))
    x1 = x_ref[...] + gt_ref[0] * _mm(merged, wo_ref[...])
    x1_ref[...] = x1
    ms = jnp.mean(x1 * x1, axis=-1, keepdims=True)
    h2 = x1 * lax.rsqrt(ms + EPS) * g_ref[...]
    h2 = h2 * (1.0 + sc_ref[0]) + sh_ref[0]
    h2_ref[...] = h2.astype(BF16)

    logits = _mm3(h2, wr_ref[...]) + br_ref[...]
    lane = lax.broadcasted_iota(jnp.int32, logits.shape, 1).astype(F32)
    gl = jnp.where((lane >= N_EXPERTS) & (lane < N_EXPERTS + N_GROUPS), logits, NEG_BIG)
    gmax, gidx = _first_index_of_max(gl, lane)
    p_g = 1.0 / jnp.sum(jnp.exp(gl - gmax), axis=-1, keepdims=True)
    first = (gidx - N_EXPERTS) * EXPERTS_PER_GROUP
    el = jnp.where((lane >= first) & (lane < first + EXPERTS_PER_GROUP), logits, NEG_BIG)
    m1, i1 = _first_index_of_max(el, lane)
    z = jnp.sum(jnp.exp(el - m1), axis=-1, keepdims=True)
    el2 = jnp.where(lane == i1, NEG_BIG, el)
    m2, i2 = _first_index_of_max(el2, lane)
    p1 = 1.0 / z
    p2 = jnp.exp(m2 - m1) / z
    tot = p1 + p2
    group_onehot = jnp.where(lane == gidx, 1.0, 0.0)
    cmb_ref[...] = (jnp.where(lane == i1, p1 / tot * p_g, 0.0)
                    + jnp.where(lane == i2, p2 / tot * p_g, 0.0) + group_onehot)
    cnt_ref[0] = jnp.broadcast_to(jnp.sum(group_onehot, axis=0, keepdims=True), cnt_ref.shape[1:])


def _post(yr, ym, sg, x2, gt, sh, sc, g, wb0, wb1, wo, wr, br, seq):
    n, d = x2.shape
    tm = 512
    per_b = seq // tm
    bmap = lambda i: (i // per_b, 0, 0)
    rowmap = lambda i: (i, 0)
    return pl.pallas_call(
        _post_kernel,
        grid=(n // tm,),
        in_specs=[pl.BlockSpec((tm, D_RWKV), rowmap),
                  pl.BlockSpec((1, D_MLA, tm), lambda i: (i // per_b, 0, i % per_b)),
                  pl.BlockSpec((tm, 2 * d), rowmap), pl.BlockSpec((tm, d), rowmap),
                  pl.BlockSpec((1, 1, d), bmap), pl.BlockSpec((1, 1, d), bmap),
                  pl.BlockSpec((1, 1, d), bmap), _const_spec((1, d)),
                  _const_spec(wb0.shape), _const_spec(wb1.shape), _const_spec(wo.shape),
                  _const_spec(wr.shape), _const_spec(br.shape)],
        out_specs=[pl.BlockSpec((tm, d), rowmap), pl.BlockSpec((tm, d), rowmap),
                   pl.BlockSpec((tm, LANES), rowmap),
                   pl.BlockSpec((1, 8, LANES), lambda i: (i, 0, 0))],
        out_shape=[jax.ShapeDtypeStruct((n, d), F32), jax.ShapeDtypeStruct((n, d), BF16),
                   jax.ShapeDtypeStruct((n, LANES), F32),
                   jax.ShapeDtypeStruct((n // tm, 8, LANES), F32)],
        compiler_params=_cparams("arbitrary"),
        name="post",
    )(yr, ym, sg, x2, gt, sh, sc, g, wb0, wb1, wo, wr, br)


MOE_TILE = 1024
MOE_CHUNK = 128


def _moe_kernel(cnt_ref, h_ref, cmb_ref, x1_ref, gt_ref, wg_ref, wu_ref, wd_ref, o_ref,
                key_scr, sel_scr, y_scr):
    i = pl.program_id(0)
    g = pl.program_id(1)
    tm = h_ref.shape[0]
    lane = lax.broadcasted_iota(jnp.int32, (1, LANES), 1)

    @pl.when(g == 0)
    def _():
        o_ref[...] = x1_ref[...]
        onehot = jnp.where((lane >= N_EXPERTS) & (lane < N_EXPERTS + N_GROUPS), cmb_ref[...], 0.0)
        before = (lax.broadcasted_iota(jnp.int32, (tm, tm), 1)
                  < lax.broadcasted_iota(jnp.int32, (tm, tm), 0))
        rank = _dot(jnp.where(before, 1.0, 0.0).astype(BF16), onehot.astype(BF16))
        key = jnp.where(onehot > 0.5, rank, -1.0)
        key_scr[...] = key.T

    key_row = key_scr[pl.ds(N_EXPERTS + g, 1), :]
    h = h_ref[...]
    cmb = cmb_ref[...]
    cmb_hi, cmb_lo = _split(cmb)
    gt = gt_ref[0]
    first_lane = g * EXPERTS_PER_GROUP

    def run_chunk(c, half):
        rows = pl.ds(half * MOE_CHUNK, MOE_CHUNK)
        slot = (c * MOE_CHUNK + lax.broadcasted_iota(jnp.int32, (MOE_CHUNK, 1), 0)).astype(F32)
        sel = jnp.where(key_row == slot, 1.0, 0.0).astype(BF16)
        xg = _dot(sel, h).astype(BF16)
        cw = _dot(sel, cmb_hi) + _dot(sel, cmb_lo)
        y = jnp.zeros((MOE_CHUNK, o_ref.shape[1]), F32)
        for e in range(EXPERTS_PER_GROUP):
            gate = _dot(xg, wg_ref[e])
            hid = gate * _sigmoid(gate) * _dot(xg, wu_ref[e])
            cwe = jnp.sum(jnp.where(lane == first_lane + e, cw, 0.0), axis=-1, keepdims=True)
            y = y + _mm(hid * cwe, wd_ref[e])
        sel_scr[rows, :] = sel
        y_scr[rows, :] = (y * gt).astype(BF16)

    n_chunks = (cnt_ref[i * N_GROUPS + g] + (MOE_CHUNK - 1)) // MOE_CHUNK

    def pair(p, carry):
        run_chunk(2 * p, 0)

        @pl.when(2 * p + 1 < n_chunks)
        def _():
            run_chunk(2 * p + 1, 1)

        @pl.when(2 * p + 1 >= n_chunks)
        def _():
            rows = pl.ds(MOE_CHUNK, MOE_CHUNK)
            sel_scr[rows, :] = jnp.zeros((MOE_CHUNK, tm), BF16)
            y_scr[rows, :] = jnp.zeros((MOE_CHUNK, o_ref.shape[1]), BF16)

        o_ref[...] += _dot_tn(sel_scr[...], y_scr[...])
        return carry

    lax.fori_loop(0, (n_chunks + 1) // 2, pair, 0)


def _moe(cnt, h2, cmb, x1, gt, wg, wu, wd, seq):
    n, d = x1.shape
    tm = MOE_TILE
    per_b = seq // tm
    rowmap = lambda i, g, cnt: (i, 0)
    wmap = lambda i, g, cnt: (g, 0, 0)
    grid_spec = pltpu.PrefetchScalarGridSpec(
        num_scalar_prefetch=1,
        grid=(n // tm, N_GROUPS),
        in_specs=[pl.BlockSpec((tm, d), rowmap), pl.BlockSpec((tm, LANES), rowmap),
                  pl.BlockSpec((tm, d), rowmap),
                  pl.BlockSpec((1, 1, d), lambda i, g, cnt: (i // per_b, 0, 0)),
                  pl.BlockSpec((EXPERTS_PER_GROUP, d, D_EXPERT), wmap),
                  pl.BlockSpec((EXPERTS_PER_GROUP, d, D_EXPERT), wmap),
                  pl.BlockSpec((EXPERTS_PER_GROUP, D_EXPERT, d), wmap)],
        out_specs=pl.BlockSpec((tm, d), rowmap),
        scratch_shapes=[pltpu.VMEM((LANES, tm), F32),
                        pltpu.VMEM((2 * MOE_CHUNK, tm), BF16),
                        pltpu.VMEM((2 * MOE_CHUNK, d), BF16)],
    )
    return pl.pallas_call(
        _moe_kernel,
        grid_spec=grid_spec,
        out_shape=jax.ShapeDtypeStruct((n, d), F32),
        compiler_params=_cparams("arbitrary", "arbitrary"),
        name="moe",
    )(cnt, h2, cmb, x1, gt, wg, wu, wd)


def _pad_cols(w, n):
    return jnp.pad(w, ((0, 0), (0, n - w.shape[1])))


def kernel(x, c, positions, w_ada, b_ada, g_norm_mix, w_in, mu_shift, w0, w_decay_up, a0, w_a_up, w_g_up, k_k, k_a, r_k, ln_x_w, ln_x_b, g_q_a, w_uq, g_kv_a, w_ukv, q_norm, k_norm, w_branch, w_out, g_norm_ffn, w_router_group, b_router_group, w_router_expert, b_router_expert, w_e_gate, w_e_up, w_e_down):
    bsz, seq, d = x.shape
    n = bsz * seq
    row = lambda a: a.reshape(1, -1)

    freqs = ROPE_THETA ** (-(jnp.arange(0, QK_ROPE_DIM, 2, dtype=F32) / QK_ROPE_DIM))
    freq = jnp.zeros((1, LANES), F32).at[0, QK_NOPE_DIM:QK_NOPE_DIM + QK_ROPE_DIM].set(
        jnp.concatenate([freqs, freqs]))
    pos = positions.astype(F32).reshape(bsz, seq, 1)

    x2 = x.reshape(n, d)
    for l in range(w_ada.shape[0]):
        mod = _ada(c, w_ada[l], b_ada[l])
        sh_m, sc_m, gt_m, sh_f, sc_f, gt_f = [m.reshape(bsz, 1, d) for m in jnp.split(mod, 6, axis=-1)]

        w_in_l = w_in[l].astype(BF16)
        wr = w_in_l[:, :RWKV_COLS]
        wm = _pad_cols(w_in_l[:, RWKV_COLS:RWKV_COLS + MLA_COLS], MLA_COLS_PAD)
        wg = w_in_l[:, RWKV_COLS + MLA_COLS:]
        zr, zm, sg = _inproj(x2, sh_m, sc_m, row(g_norm_mix[l]), wr, wm, wg, seq)

        zeros_lora = jnp.zeros((DECAY_LORA, D_RWKV), F32)
        wdec = jnp.concatenate([w_decay_up[l], zeros_lora], axis=0)
        wa = jnp.concatenate([zeros_lora, w_a_up[l]], axis=0)
        prep = _rwkv_prep(zr.reshape(bsz, seq, RWKV_COLS), row(mu_shift[l]), row(w0[l]), wdec,
                          row(a0[l]), wa, w_g_up[l], row(k_k[l]), row(k_a[l]), row(r_k[l]))
        y_rwkv = _rwkv_scan(*prep, row(ln_x_w[l]), row(ln_x_b[l]))

        wuq = w_uq[l].reshape(Q_LORA, MLA_HEADS, QK_HEAD_DIM)
        wuq = jnp.pad(wuq, ((0, 0), (0, 0), (0, HEAD_PAD - QK_HEAD_DIM))).reshape(Q_LORA, -1)
        qn = _pad_cols(row(q_norm[l]), LANES)
        kn = _pad_cols(row(k_norm[l]), LANES)
        bound = (QK_HEAD_DIM * jnp.max(jnp.abs(q_norm[l])) * jnp.max(jnp.abs(k_norm[l]))
                 * QK_HEAD_DIM ** -0.5 * LOG2_E)
        fixed = 2.0 * bound <= MAX_FIXED_SHIFT_SPAN
        kshift = jnp.zeros((1, LANES), F32).at[0, QK_HEAD_DIM].set(jnp.where(fixed, -bound, 0.0))
        q, k, v = _mla_prep(zm.reshape(bsz, seq, MLA_COLS_PAD), pos, freq, row(g_q_a[l]),
                            wuq.astype(BF16), row(g_kv_a[l]), w_ukv[l].astype(BF16), qn, kn, kshift)
        y_mla = _attn(q, k, v, fixed.astype(jnp.int32).reshape(1))

        w_router = _pad_cols(jnp.concatenate([w_router_expert[l], w_router_group[l]], axis=1), LANES)
        b_router = _pad_cols(row(jnp.concatenate([b_router_expert[l], b_router_group[l]])), LANES)
        x1, h2, cmb, cnt = _post(y_rwkv.reshape(n, D_RWKV), y_mla, sg, x2,
                                 gt_m, sh_f, sc_f, row(g_norm_ffn[l]),
                                 w_branch[l, 0].astype(BF16), w_branch[l, 1].astype(BF16),
                                 w_out[l].astype(BF16), w_router, b_router, seq)

        cnt = cnt[:, 0, N_EXPERTS:N_EXPERTS + N_GROUPS].reshape(n // MOE_TILE, -1, N_GROUPS)
        cnt = jnp.sum(cnt, axis=1).astype(jnp.int32).reshape(-1)
        x2 = _moe(cnt, h2, cmb, x1, gt_f, w_e_gate[l].astype(BF16), w_e_up[l].astype(BF16),
                  w_e_down[l].astype(BF16), seq)
    return x2.reshape(bsz, seq, d)
```

```python
import functools

import numpy as np
import jax
import jax.numpy as jnp
from jax import lax
from jax.experimental import pallas as pl
from jax.experimental.pallas import tpu as pltpu

F32 = jnp.float32
BF16 = jnp.bfloat16

D_MODEL = 1024
EPS = 1e-6
RWKV_HEADS = 8
RWKV_HEAD_DIM = 64
D_RWKV = 512
DECAY_LORA = 64
AAA_LORA = 64
GATE_LORA = 128
LN_X_EPS = 64e-5
MLA_HEADS = 8
QK_NOPE_DIM = 64
QK_ROPE_DIM = 32
QK_HEAD_DIM = 96
V_HEAD_DIM = 64
D_MLA = 512
Q_LORA = 256
KV_LORA = 128
ROPE_THETA = 10000.0
RWKV_COLS = 1792
MLA_COLS = 416
MLA_COLS_PAD = 512
N_GROUPS = 4
EXPERTS_PER_GROUP = 8
N_EXPERTS = 32
D_EXPERT = 256

LANES = 128
CHUNK = 64
HEAD_PAD = 128
NEG_BIG = -1e30
LOG2_E = 1.4426950408889634
MAX_FIXED_SHIFT_SPAN = 80.0
VMEM_LIMIT = 56 * 1024 * 1024


def _cparams(*sem):
    return pltpu.CompilerParams(dimension_semantics=sem, vmem_limit_bytes=VMEM_LIMIT)


def _dot(a, b):
    return jnp.dot(a, b, preferred_element_type=F32)


def _dot_nt(a, b):
    return lax.dot_general(a, b, (((1,), (1,)), ((), ())), preferred_element_type=F32)


def _dot_tn(a, b):
    return lax.dot_general(a, b, (((0,), (0,)), ((), ())), preferred_element_type=F32)


def _mm(a, b):
    return _dot(a.astype(BF16), b.astype(BF16))


def _split(a):
    hi = a.astype(BF16)
    lo = (a - hi.astype(F32)).astype(BF16)
    return hi, lo


def _mm_rhs_exact(a, b):
    hi, lo = _split(a)
    return _dot(hi, b) + _dot(lo, b)


def _mm_lhs_exact(a, b):
    hi, lo = _split(b)
    return _dot(a, hi) + _dot(a, lo)


def _mm3(a, b):
    ah, al = _split(a)
    bh, bl = _split(b)
    return _dot(ah, bh) + (_dot(ah, bl) + _dot(al, bh))


def _sigmoid(x):
    return 1.0 / (1.0 + jnp.exp(-x))


def _const_spec(shape):
    nd = len(shape)
    return pl.BlockSpec(shape, lambda *_: (0,) * nd)


def _ada_kernel(c_ref, w_ref, b_ref, o_ref):
    c = c_ref[...]
    s = c * _sigmoid(c)
    o_ref[...] = _mm3(s, w_ref[...]) + b_ref[...]


def _ada(c, w, b):
    bsz, d = c.shape
    n = w.shape[1]
    tn = 1024
    return pl.pallas_call(
        _ada_kernel,
        grid=(n // tn,),
        in_specs=[pl.BlockSpec((bsz, d), lambda j: (0, 0)),
                  pl.BlockSpec((d, tn), lambda j: (0, j)),
                  pl.BlockSpec((1, tn), lambda j: (0, j))],
        out_specs=pl.BlockSpec((bsz, tn), lambda j: (0, j)),
        out_shape=jax.ShapeDtypeStruct((bsz, n), F32),
        compiler_params=_cparams("arbitrary"),
        name="ada",
    )(c, w, b.reshape(1, n))


def _inproj_kernel(x_ref, sh_ref, sc_ref, g_ref, wr_ref, wm_ref, wg_ref,
                   zr_ref, zm_ref, sg_ref):
    x = x_ref[...]
    ms = jnp.mean(x * x, axis=-1, keepdims=True)
    h = x * lax.rsqrt(ms + EPS) * g_ref[...]
    h = h * (1.0 + sc_ref[0]) + sh_ref[0]
    hb = h.astype(BF16)
    zr_ref[...] = _dot(hb, wr_ref[...])
    zm_ref[...] = _dot(hb, wm_ref[...])
    sg_ref[...] = _sigmoid(_dot(hb, wg_ref[...])).astype(BF16)


def _inproj(x2, sh, sc, g, wr, wm, wg, seq):
    n, d = x2.shape
    tm = 512
    per_b = seq // tm
    bmap = lambda i: (i // per_b, 0, 0)
    return pl.pallas_call(
        _inproj_kernel,
        grid=(n // tm,),
        in_specs=[pl.BlockSpec((tm, d), lambda i: (i, 0)),
                  pl.BlockSpec((1, 1, d), bmap),
                  pl.BlockSpec((1, 1, d), bmap),
                  _const_spec((1, d)),
                  _const_spec(wr.shape), _const_spec(wm.shape), _const_spec(wg.shape)],
        out_specs=[pl.BlockSpec((tm, wr.shape[1]), lambda i: (i, 0)),
                   pl.BlockSpec((tm, wm.shape[1]), lambda i: (i, 0)),
                   pl.BlockSpec((tm, wg.shape[1]), lambda i: (i, 0))],
        out_shape=[jax.ShapeDtypeStruct((n, wr.shape[1]), F32),
                   jax.ShapeDtypeStruct((n, wm.shape[1]), F32),
                   jax.ShapeDtypeStruct((n, wg.shape[1]), BF16)],
        compiler_params=_cparams("arbitrary"),
        name="inproj",
    )(x2, sh, sc, g, wr, wm, wg)


def _rwkv_prep_kernel(z_ref, prev_ref, mu_ref, w0_ref, wdec_ref, a0_ref, wa_ref, wgu_ref,
                      kk_ref, ka_ref, rk_ref, ones_ref, tri_ref, blk_ref,
                      rp_ref, am_ref, bm_ref, km_ref, bh_ref, kh_ref, v_ref,
                      pc_ref, bv_ref, g_ref):
    tt = z_ref.shape[1]
    z = z_ref[0]
    prev = prev_ref[0][7:8, :]
    prev = jnp.where(pl.program_id(1) == 0, 0.0, prev)
    row = lax.broadcasted_iota(jnp.int32, (tt, 1), 0)
    zs = jnp.where(row == 0, prev, pltpu.roll(z, 1, axis=0))
    z = z + (zs - z) * mu_ref[...]
    zr = z[:, 0:D_RWKV]
    zk = z[:, D_RWKV:2 * D_RWKV]
    zv = z[:, 2 * D_RWKV:3 * D_RWKV]
    zwa = z[:, 3 * D_RWKV:3 * D_RWKV + DECAY_LORA + AAA_LORA]
    zg = z[:, 3 * D_RWKV + DECAY_LORA + AAA_LORA:]

    u = -(w0_ref[...] + _mm3(jnp.tanh(zwa), wdec_ref[...]))
    softplus = jnp.maximum(u, 0.0) + jnp.log(1.0 + jnp.exp(-jnp.abs(u)))
    logw = -jnp.exp(-softplus - 0.5)
    a = _sigmoid(a0_ref[...] + _mm(zwa, wa_ref[...]))
    g = _mm(_sigmoid(zg), wgu_ref[...])

    ones_blk = ones_ref[...]
    xk = zk * kk_ref[...]
    ss = _mm_rhs_exact(xk * xk, ones_blk)
    kk = xk * lax.rsqrt(jnp.maximum(ss, 1e-24))
    k = zk * (1.0 + (a - 1.0) * ka_ref[...])
    b = kk * a
    bonus = _mm_rhs_exact(zr * k * rk_ref[...], ones_blk)

    cum = _mm_lhs_exact(tri_ref[...], logw)
    tot = _mm_lhs_exact(blk_ref[...], logw)
    e_neg = jnp.exp(-cum)
    e_tot = jnp.exp(tot)
    bm = b * e_neg
    km = k * e_neg
    rp_ref[0] = (zr * jnp.exp(cum)).astype(BF16)
    am_ref[0] = (-kk * jnp.exp(cum - logw)).astype(BF16)
    bm_ref[0] = bm.astype(BF16)
    km_ref[0] = km.astype(BF16)
    bh_ref[0] = (bm * e_tot).astype(BF16)
    kh_ref[0] = (km * e_tot).astype(BF16)
    v_ref[0] = zv.astype(BF16)
    bv_ref[0] = bonus * zv
    g_ref[0] = g
    nc = tt // CHUNK
    pick = (lax.broadcasted_iota(jnp.int32, (nc, tt), 1)
            == CHUNK * lax.broadcasted_iota(jnp.int32, (nc, tt), 0))
    pc_ref[0] = _mm_lhs_exact(jnp.where(pick, 1.0, 0.0).astype(BF16), e_tot)


def _rwkv_prep(zr3, mu, w0, wdec, a0, wa, wgu, k_k, k_a, r_k):
    bsz, seq, cols = zr3.shape
    tt = 512
    d = D_RWKV
    ones_blk = jnp.asarray(np.kron(np.eye(RWKV_HEADS), np.ones((RWKV_HEAD_DIM, RWKV_HEAD_DIM))), BF16)
    cidx = np.arange(tt) // CHUNK
    same = cidx[:, None] == cidx[None, :]
    tri = jnp.asarray(same & (np.arange(tt)[None, :] <= np.arange(tt)[:, None]), BF16)
    blk = jnp.asarray(same, BF16)
    tok = lambda b, j: (b, j, 0)
    big = pl.BlockSpec((1, tt, d), tok)
    out_bf = jax.ShapeDtypeStruct((bsz, seq, d), BF16)
    out_f = jax.ShapeDtypeStruct((bsz, seq, d), F32)
    return pl.pallas_call(
        _rwkv_prep_kernel,
        grid=(bsz, seq // tt),
        in_specs=[pl.BlockSpec((1, tt, cols), tok),
                  pl.BlockSpec((1, 8, cols), lambda b, j: (b, jnp.maximum(j * (tt // 8) - 1, 0), 0)),
                  _const_spec((1, cols)), _const_spec((1, d)), _const_spec(wdec.shape),
                  _const_spec((1, d)), _const_spec(wa.shape), _const_spec(wgu.shape),
                  _const_spec((1, d)), _const_spec((1, d)), _const_spec((1, d)),
                  _const_spec((d, d)), _const_spec((tt, tt)), _const_spec((tt, tt))],
        out_specs=[big] * 7 + [pl.BlockSpec((1, tt // CHUNK, d), tok), big, big],
        out_shape=[out_bf] * 7 + [jax.ShapeDtypeStruct((bsz, seq // CHUNK, d), F32), out_f, out_f],
        compiler_params=_cparams("arbitrary", "arbitrary"),
        name="rwkv_prep",
    )(zr3, zr3, mu, w0, wdec, a0, wa, wgu, k_k, k_a, r_k, ones_blk, tri, blk)


def _rwkv_scan_kernel(rp_ref, am_ref, bm_ref, km_ref, bh_ref, kh_ref, v_ref, pc_ref,
                      bv_ref, g_ref, lnw_ref, lnb_ref, o_ref, h_scr, y_scr):
    tb = rp_ref.shape[1]
    nc = tb // CHUNK
    pp = rp_ref.shape[2] // LANES
    first_pair = pl.program_id(2) * pp

    @pl.when(pl.program_id(1) == 0)
    def _():
        for p in range(pp):
            h_scr[first_pair + p] = jnp.zeros((LANES, LANES), F32)

    lane = lax.broadcasted_iota(jnp.int32, (1, LANES), 1)
    lane2 = lax.broadcasted_iota(jnp.int32, (1, 2 * LANES), 1)
    in_head = [lane < RWKV_HEAD_DIM, lane >= RWKV_HEAD_DIM]
    in_head2 = [(lane2 % LANES) < RWKV_HEAD_DIM, (lane2 % LANES) >= RWKV_HEAD_DIM]

    def blk(x):
        masks = in_head if x.shape[1] == LANES else in_head2
        zero = jnp.zeros((), x.dtype)
        return jnp.concatenate([jnp.where(masks[0], x, zero), jnp.where(masks[1], x, zero)], axis=0)

    rr = lax.broadcasted_iota(jnp.int32, (CHUNK, LANES), 0)
    cc = lax.broadcasted_iota(jnp.int32, (CHUNK, LANES), 1) % CHUNK
    eye_packed = cc == rr
    r2 = lax.broadcasted_iota(jnp.int32, (2 * CHUNK, LANES), 0)
    c2 = lax.broadcasted_iota(jnp.int32, (2 * CHUNK, LANES), 1) % CHUNK
    causal2 = ((r2 < CHUNK) & (c2 < r2)) | ((r2 >= CHUNK) & (c2 <= r2 - CHUNK))
    level_masks = [((rr >> (k + 1)) == (cc >> (k + 1))) & ((rr >> k) == (cc >> k) + 1)
                   for k in range(6)]
    r128 = lax.broadcasted_iota(jnp.int32, (LANES, LANES), 0)
    c128 = lax.broadcasted_iota(jnp.int32, (LANES, LANES), 1)
    blockdiag = (r128 < RWKV_HEAD_DIM) == (c128 < RWKV_HEAD_DIM)
    eye = r128 == c128

    slabs = [(p, c) for p in range(pp) for c in range(nc)]

    def load(ref, p, c):
        return ref[0, pl.ds(c * CHUNK, CHUNK), pl.ds(p * LANES, LANES)]

    rp = {s: load(rp_ref, *s) for s in slabs}
    am = {s: load(am_ref, *s) for s in slabs}
    v = {s: load(v_ref, *s) for s in slabs}

    a_b, a_k = {}, {}
    for s in slabs:
        lhs = jnp.concatenate([am[s], rp[s]], axis=0)
        a_b[s] = jnp.where(causal2, _dot_nt(lhs, blk(load(bm_ref, *s))), 0.0)
        a_k[s] = jnp.where(causal2, _dot_nt(lhs, blk(load(km_ref, *s))), 0.0)
    akv = {s: _dot(a_k[s].astype(BF16), blk(v[s])) for s in slabs}

    a_ab = {s: a_b[s][:CHUNK] for s in slabs}
    tinv = {s: jnp.where(eye_packed, 1.0, 0.0) + jnp.where(level_masks[0], a_ab[s], 0.0)
            for s in slabs}
    for lm in level_masks[1:]:
        tbf = {s: tinv[s].astype(BF16) for s in slabs}
        inner = {s: _dot(jnp.where(lm, a_ab[s], 0.0).astype(BF16), blk(tbf[s])) for s in slabs}
        tinv = {s: tinv[s] + _dot(tbf[s], blk(inner[s].astype(BF16))) for s in slabs}
    z = {s: _dot(tinv[s].astype(BF16),
                 blk(jnp.concatenate([am[s], akv[s][:CHUNK].astype(BF16)], axis=1)))
         for s in slabs}
    g2 = {s: _dot(a_b[s][CHUNK:].astype(BF16), blk(z[s].astype(BF16))) for s in slabs}

    gy, m, hadd = {}, {}, {}
    for s in slabs:
        p, c = s
        mh = _dot_tn(load(bh_ref, p, c), z[s].astype(BF16))
        kv = _dot_tn(load(kh_ref, p, c), v[s])
        pc = pc_ref[0, c:c + 1, pl.ds(p * LANES, LANES)]
        m[s] = (jnp.where(blockdiag, mh[:, :LANES], 0.0) + jnp.where(eye, pc, 0.0)).astype(BF16)
        hadd[s] = jnp.where(blockdiag, mh[:, LANES:] + kv, 0.0)
        gy[s] = ((rp[s].astype(F32) + g2[s][:, :LANES]).astype(BF16),
                 g2[s][:, LANES:] + akv[s][CHUNK:])

    hs = [h_scr[first_pair + p] for p in range(pp)]
    for c in range(nc):
        for p in range(pp):
            hb = hs[p].astype(BF16)
            gmat, y0 = gy[p, c]
            y_scr[pl.ds(c * CHUNK, CHUNK), pl.ds(p * LANES, LANES)] = _dot(gmat, hb) + y0
            hs[p] = _dot(m[p, c], hb) + hadd[p, c]
    for p in range(pp):
        h_scr[first_pair + p] = hs[p]

    avg = jnp.where(blockdiag, 1.0 / RWKV_HEAD_DIM, 0.0).astype(BF16)
    for p in range(pp):
        cols = pl.ds(p * LANES, LANES)
        y = y_scr[:, cols]
        mean = _mm_rhs_exact(y, avg)
        yc = y - mean
        var = _mm_rhs_exact(yc * yc, avg)
        yn = yc * lax.rsqrt(var + LN_X_EPS)
        out = (yn * lnw_ref[:, cols] + lnb_ref[:, cols] + bv_ref[0, :, cols]) * g_ref[0, :, cols]
        o_ref[0, :, cols] = out.astype(BF16)


def _rwkv_scan(rp, am, bm, km, bh, kh, v, pc, bv, g, lnw, lnb):
    bsz, seq, d = rp.shape
    tb = 512
    npairs = d // LANES
    pp = 4
    width = pp * LANES
    tok = pl.BlockSpec((1, tb, width), lambda b, t, p: (b, t, p))
    vec = pl.BlockSpec((1, width), lambda b, t, p: (0, p))
    return pl.pallas_call(
        _rwkv_scan_kernel,
        grid=(bsz, seq // tb, npairs // pp),
        in_specs=[tok] * 7 + [pl.BlockSpec((1, tb // CHUNK, width), lambda b, t, p: (b, t, p)),
                              tok, tok, vec, vec],
        out_specs=tok,
        out_shape=jax.ShapeDtypeStruct((bsz, seq, d), BF16),
        scratch_shapes=[pltpu.VMEM((npairs, LANES, LANES), F32),
                        pltpu.VMEM((tb, width), F32)],
        compiler_params=_cparams("arbitrary", "arbitrary", "arbitrary"),
        name="rwkv_scan",
    )(rp, am, bm, km, bh, kh, v, pc, bv, g, lnw, lnb)


def _rot_half(a, axis=-1):
    half = QK_ROPE_DIM // 2
    lo = lax.slice_in_dim(a, QK_NOPE_DIM, QK_NOPE_DIM + half, axis=axis)
    hi = lax.slice_in_dim(a, QK_NOPE_DIM + half, QK_NOPE_DIM + QK_ROPE_DIM, axis=axis)
    pads = [(0, 0)] * a.ndim
    pads[axis] = (QK_NOPE_DIM, a.shape[axis] - QK_NOPE_DIM - QK_ROPE_DIM)
    return jnp.pad(jnp.concatenate([-hi, lo], axis=axis), pads)


def _mla_prep_kernel(z_ref, pos_ref, freq_ref, gq_ref, wuq_ref, wuqr_ref, gkv_ref, wukv_ref, wvt_ref,
                     qn_ref, qnr_ref, kn_ref, knr_ref, kshift_ref, q_ref, k_ref, vt_ref):
    z = z_ref[0]
    zq = z[:, :Q_LORA]
    zkv = z[:, Q_LORA:Q_LORA + KV_LORA]
    zpe = z[:, Q_LORA + KV_LORA:]
    ang = pos_ref[0] * freq_ref[...]
    cos = jnp.cos(ang)
    sin = jnp.sin(ang)
    lane = lax.broadcasted_iota(jnp.int32, (1, LANES), 1)
    nope = lane < QK_NOPE_DIM
    half = QK_ROPE_DIM // 2

    qa = (zq * lax.rsqrt(jnp.mean(zq * zq, axis=-1, keepdims=True) + EPS) * gq_ref[...]).astype(BF16)
    q_all = _dot(qa, wuq_ref[...])
    qr_all = _dot(qa, wuqr_ref[...])
    kva = (zkv * lax.rsqrt(jnp.mean(zkv * zkv, axis=-1, keepdims=True) + EPS) * gkv_ref[...]).astype(BF16)
    kv_all = _dot(kva, wukv_ref[...])
    pe = pltpu.roll(zpe, QK_NOPE_DIM, axis=1)
    pe_rot = jnp.where(lane < QK_NOPE_DIM + half,
                       -pltpu.roll(pe, LANES - half, axis=1), pltpu.roll(pe, half, axis=1))
    scale = QK_HEAD_DIM ** -0.5 * LOG2_E
    qc = qn_ref[...] * scale * cos
    qs = qnr_ref[...] * scale * sin
    kc = kn_ref[...] * cos
    ks = knr_ref[...] * sin
    pe_term = pe_rot * ks
    ones_row = lax.broadcasted_iota(jnp.int32, (LANES, 1), 0) == V_HEAD_DIM
    for h in range(MLA_HEADS):
        cols = slice(h * HEAD_PAD, (h + 1) * HEAD_PAD)
        q = q_all[:, cols]
        r = lax.rsqrt(jnp.sum(q * q, axis=-1, keepdims=True) * (1.0 / QK_HEAD_DIM) + EPS)
        q = r * (q * qc + qr_all[:, cols] * qs)
        q_ref[0, h] = jnp.where(lane == QK_HEAD_DIM, 1.0, q).astype(BF16)
        k = jnp.where(nope, kv_all[:, cols], pe)
        r = lax.rsqrt(jnp.sum(k * k, axis=-1, keepdims=True) * (1.0 / QK_HEAD_DIM) + EPS)
        k_ref[0, h] = (r * (k * kc + pe_term) + kshift_ref[...]).astype(BF16)
        vt = _dot_nt(wvt_ref[h], kva)
        vt_ref[0, h] = jnp.where(ones_row, 1.0, vt).astype(BF16)


def _mla_prep(zm3, pos, freq, gq, wuq, gkv, wukv, qn, kn, kshift):
    bsz, seq, cols = zm3.shape
    tm = 512
    tok = lambda b, j: (b, j, 0)
    head = lambda b, j: (b, 0, j, 0)
    wuqr = _rot_half(wuq.reshape(Q_LORA, MLA_HEADS, HEAD_PAD)).reshape(Q_LORA, -1)
    unsign = _rot_half(jnp.ones((1, LANES), F32))
    qnr = _rot_half(qn) * unsign
    knr = _rot_half(kn) * unsign
    wv = wukv.reshape(KV_LORA, MLA_HEADS, HEAD_PAD)[:, :, QK_NOPE_DIM:]
    wvt = jnp.pad(wv.transpose(1, 2, 0), ((0, 0), (0, HEAD_PAD - V_HEAD_DIM), (0, 0)))
    return pl.pallas_call(
        _mla_prep_kernel,
        grid=(bsz, seq // tm),
        in_specs=[pl.BlockSpec((1, tm, cols), tok),
                  pl.BlockSpec((1, tm, 1), tok),
                  _const_spec(freq.shape), _const_spec((1, Q_LORA)), _const_spec(wuq.shape),
                  _const_spec(wuqr.shape), _const_spec((1, KV_LORA)), _const_spec(wukv.shape),
                  _const_spec(wvt.shape), _const_spec((1, LANES)), _const_spec((1, LANES)),
                  _const_spec((1, LANES)), _const_spec((1, LANES)), _const_spec((1, LANES))],
        out_specs=[pl.BlockSpec((1, MLA_HEADS, tm, HEAD_PAD), head),
                   pl.BlockSpec((1, MLA_HEADS, tm, HEAD_PAD), head),
                   pl.BlockSpec((1, MLA_HEADS, HEAD_PAD, tm), lambda b, j: (b, 0, 0, j))],
        out_shape=[jax.ShapeDtypeStruct((bsz, MLA_HEADS, seq, HEAD_PAD), BF16),
                   jax.ShapeDtypeStruct((bsz, MLA_HEADS, seq, HEAD_PAD), BF16),
                   jax.ShapeDtypeStruct((bsz, MLA_HEADS, HEAD_PAD, seq), BF16)],
        compiler_params=_cparams("arbitrary", "arbitrary"),
        name="mla_prep",
    )(zm3, pos, freq, gq, wuq, wuqr, gkv, wukv, wvt, qn, qnr, kn, knr, kshift)


def _attn_kernel(qi_ref, kj_ref, fixed_ref, q_ref, k_ref, vt_ref, o_ref, m_scr, acc_scr):
    p = pl.program_id(1)
    qi = qi_ref[p]
    kj = kj_ref[p]
    fixed_shift = fixed_ref[0] == 1
    bq = q_ref.shape[2]
    bk = k_ref.shape[2]
    heads = range(MLA_HEADS)

    @pl.when(kj == 0)
    def _():
        m_scr[...] = jnp.full(m_scr.shape, NEG_BIG, F32)
        acc_scr[...] = jnp.zeros(acc_scr.shape, F32)

    def scores(diagonal):
        st = [_dot_nt(k_ref[0, h], q_ref[0, h]) for h in heads]
        if diagonal:
            visible = (lax.broadcasted_iota(jnp.int32, (bk, bq), 0)
                       <= lax.broadcasted_iota(jnp.int32, (bk, bq), 1))
            st = [jnp.where(visible, x, NEG_BIG) for x in st]
        return st

    def block_fixed(diagonal):
        st = scores(diagonal)
        pr = [jnp.exp2(st[h]).astype(BF16) for h in heads]
        pv = [_dot(vt_ref[0, h], pr[h]) for h in heads]
        for h in heads:
            acc_scr[h] += pv[h]

    def block(diagonal):
        st = scores(diagonal)
        m_prev = [m_scr[h] for h in heads]
        m_new = [jnp.maximum(m_prev[h], jnp.max(st[h], axis=0, keepdims=True)) for h in heads]
        pr = [jnp.exp2(st[h] - m_new[h]).astype(BF16) for h in heads]
        alpha = [jnp.exp2(m_prev[h] - m_new[h]) for h in heads]
        pv = [_dot(vt_ref[0, h], pr[h]) for h in heads]
        for h in heads:
            acc_scr[h] = alpha[h] * acc_scr[h] + pv[h]
            m_scr[h] = m_new[h]

    for diagonal in (False, True):
        on_diag = (kj == qi) if diagonal else (kj != qi)
        pl.when(on_diag & fixed_shift)(functools.partial(block_fixed, diagonal))
        pl.when(on_diag & jnp.logical_not(fixed_shift))(functools.partial(block, diagonal))

    @pl.when(kj == qi)
    def _():
        for h in heads:
            acc = acc_scr[h]
            o = acc[:V_HEAD_DIM] * (1.0 / acc[V_HEAD_DIM:V_HEAD_DIM + 1])
            o_ref[0, h * V_HEAD_DIM:(h + 1) * V_HEAD_DIM, :] = o.astype(BF16)


def _attn(q, k, v, fixed):
    bsz, nh, seq, _ = q.shape
    bq = bk = 512
    nq = seq // bq
    qi = np.concatenate([np.full(i + 1, i) for i in range(nq)]).astype(np.int32)
    kj = np.concatenate([np.arange(i + 1) for i in range(nq)]).astype(np.int32)
    grid_spec = pltpu.PrefetchScalarGridSpec(
        num_scalar_prefetch=3,
        grid=(bsz, len(qi)),
        in_specs=[pl.BlockSpec((1, nh, bq, HEAD_PAD), lambda b, p, qi, kj, f: (b, 0, qi[p], 0)),
                  pl.BlockSpec((1, nh, bk, HEAD_PAD), lambda b, p, qi, kj, f: (b, 0, kj[p], 0)),
                  pl.BlockSpec((1, nh, HEAD_PAD, bk), lambda b, p, qi, kj, f: (b, 0, 0, kj[p]))],
        out_specs=pl.BlockSpec((1, nh * V_HEAD_DIM, bq), lambda b, p, qi, kj, f: (b, 0, qi[p])),
        scratch_shapes=[pltpu.VMEM((nh, 1, bq), F32), pltpu.VMEM((nh, HEAD_PAD, bq), F32)],
    )
    return pl.pallas_call(
        _attn_kernel,
        grid_spec=grid_spec,
        out_shape=jax.ShapeDtypeStruct((bsz, nh * V_HEAD_DIM, seq), BF16),
        compiler_params=_cparams("arbitrary", "arbitrary"),
        name="mla_attn",
    )(jnp.asarray(qi), jnp.asarray(kj), fixed, q, k, v)


def _first_index_of_max(vals, lane):
    mx = jnp.max(vals, axis=-1, keepdims=True)
    idx = jnp.min(jnp.where(vals == mx, lane, LANES), axis=-1, keepdims=True)
    return mx, idx


def _post_kernel(yr_ref, ymt_ref, sg_ref, x_ref, gt_ref, sh_ref, sc_ref, g_ref,
                 wb0_ref, wb1_ref, wo_ref, wr_ref, br_ref, x1_ref, h2_ref, cmb_ref, cnt_ref):
    sg = sg_ref[...].astype(F32)
    merged = (sg[:, :D_MODEL] * _dot(yr_ref[...], wb0_ref[...])
              + sg[:, D_MODEL:] * _dot_tn(ymt_ref[0], wb1_ref[...]))
    x1 = x_ref[...] + gt_ref[0] * _mm(merged, wo_ref[...])
    x1_ref[...] = x1
    ms = jnp.mean(x1 * x1, axis=-1, keepdims=True)
    h2 = x1 * lax.rsqrt(ms + EPS) * g_ref[...]
    h2 = h2 * (1.0 + sc_ref[0]) + sh_ref[0]
    h2_ref[...] = h2.astype(BF16)

    logits = _mm3(h2, wr_ref[...]) + br_ref[...]
    lane = lax.broadcasted_iota(jnp.int32, logits.shape, 1).astype(F32)
    gl = jnp.where((lane >= N_EXPERTS) & (lane < N_EXPERTS + N_GROUPS), logits, NEG_BIG)
    gmax, gidx = _first_index_of_max(gl, lane)
    p_g = 1.0 / jnp.sum(jnp.exp(gl - gmax), axis=-1, keepdims=True)
    first = (gidx - N_EXPERTS) * EXPERTS_PER_GROUP
    el = jnp.where((lane >= first) & (lane < first + EXPERTS_PER_GROUP), logits, NEG_BIG)
    m1, i1 = _first_index_of_max(el, lane)
    z = jnp.sum(jnp.exp(el - m1), axis=-1, keepdims=True)
    el2 = jnp.where(lane == i1, NEG_BIG, el)
    m2, i2 = _first_index_of_max(el2, lane)
    p1 = 1.0 / z
    p2 = jnp.exp(m2 - m1) / z
    tot = p1 + p2
    group_onehot = jnp.where(lane == gidx, 1.0, 0.0)
    cmb_ref[...] = (jnp.where(lane == i1, p1 / tot * p_g, 0.0)
                    + jnp.where(lane == i2, p2 / tot * p_g, 0.0) + group_onehot)
    cnt_ref[0] = jnp.broadcast_to(jnp.sum(group_onehot, axis=0, keepdims=True), cnt_ref.shape[1:])


def _post(yr, ym, sg, x2, gt, sh, sc, g, wb0, wb1, wo, wr, br, seq):
    n, d = x2.shape
    tm = 512
    per_b = seq // tm
    bmap = lambda i: (i // per_b, 0, 0)
    rowmap = lambda i: (i, 0)
    return pl.pallas_call(
        _post_kernel,
        grid=(n // tm,),
        in_specs=[pl.BlockSpec((tm, D_RWKV), rowmap),
                  pl.BlockSpec((1, D_MLA, tm), lambda i: (i // per_b, 0, i % per_b)),
                  pl.BlockSpec((tm, 2 * d), rowmap), pl.BlockSpec((tm, d), rowmap),
                  pl.BlockSpec((1, 1, d), bmap), pl.BlockSpec((1, 1, d), bmap),
                  pl.BlockSpec((1, 1, d), bmap), _const_spec((1, d)),
                  _const_spec(wb0.shape), _const_spec(wb1.shape), _const_spec(wo.shape),
                  _const_spec(wr.shape), _const_spec(br.shape)],
        out_specs=[pl.BlockSpec((tm, d), rowmap), pl.BlockSpec((tm, d), rowmap),
                   pl.BlockSpec((tm, LANES), rowmap),
                   pl.BlockSpec((1, 8, LANES), lambda i: (i, 0, 0))],
        out_shape=[jax.ShapeDtypeStruct((n, d), F32), jax.ShapeDtypeStruct((n, d), BF16),
                   jax.ShapeDtypeStruct((n, LANES), F32),
                   jax.ShapeDtypeStruct((n // tm, 8, LANES), F32)],
        compiler_params=_cparams("arbitrary"),
        name="post",
    )(yr, ym, sg, x2, gt, sh, sc, g, wb0, wb1, wo, wr, br)


MOE_TILE = 1024
MOE_CHUNK = 256


def _moe_kernel(cnt_ref, h_ref, cmb_ref, x1_ref, gt_ref, wg_ref, wu_ref, wd_ref, o_ref,
                key_scr):
    i = pl.program_id(0)
    g = pl.program_id(1)
    tm = h_ref.shape[0]
    lane = lax.broadcasted_iota(jnp.int32, (1, LANES), 1)

    @pl.when(g == 0)
    def _():
        o_ref[...] = x1_ref[...]
        onehot = jnp.where((lane >= N_EXPERTS) & (lane < N_EXPERTS + N_GROUPS), cmb_ref[...], 0.0)
        before = (lax.broadcasted_iota(jnp.int32, (tm, tm), 1)
                  < lax.broadcasted_iota(jnp.int32, (tm, tm), 0))
        rank = _dot(jnp.where(before, 1.0, 0.0).astype(BF16), onehot.astype(BF16))
        key = jnp.where(onehot > 0.5, rank, -1.0)
        key_scr[...] = key.T

    key_row = key_scr[pl.ds(N_EXPERTS + g, 1), :]
    h = h_ref[...]
    cmb = cmb_ref[...]
    cmb_hi, cmb_lo = _split(cmb)
    gt = gt_ref[0]
    first_lane = g * EXPERTS_PER_GROUP

    def chunk(c, carry):
        slot = (c * MOE_CHUNK + lax.broadcasted_iota(jnp.int32, (MOE_CHUNK, 1), 0)).astype(F32)
        sel = jnp.where(key_row == slot, 1.0, 0.0).astype(BF16)
        xg = _dot(sel, h).astype(BF16)
        cw = _dot(sel, cmb_hi) + _dot(sel, cmb_lo)
        y = jnp.zeros((MOE_CHUNK, o_ref.shape[1]), F32)
        for e in range(EXPERTS_PER_GROUP):
            gate = _dot(xg, wg_ref[e])
            hid = gate * _sigmoid(gate) * _dot(xg, wu_ref[e])
            cwe = jnp.sum(jnp.where(lane == first_lane + e, cw, 0.0), axis=-1, keepdims=True)
            y = y + _mm(hid * cwe, wd_ref[e])
        o_ref[...] += _dot_tn(sel, (y * gt).astype(BF16))
        return carry

    n_chunks = (cnt_ref[i * N_GROUPS + g] + (MOE_CHUNK - 1)) // MOE_CHUNK
    lax.fori_loop(0, n_chunks, chunk, 0)


def _moe(cnt, h2, cmb, x1, gt, wg, wu, wd, seq):
    n, d = x1.shape
    tm = MOE_TILE
    per_b = seq // tm
    rowmap = lambda i, g, cnt: (i, 0)
    wmap = lambda i, g, cnt: (g, 0, 0)
    grid_spec = pltpu.PrefetchScalarGridSpec(
        num_scalar_prefetch=1,
        grid=(n // tm, N_GROUPS),
        in_specs=[pl.BlockSpec((tm, d), rowmap), pl.BlockSpec((tm, LANES), rowmap),
                  pl.BlockSpec((tm, d), rowmap),
                  pl.BlockSpec((1, 1, d), lambda i, g, cnt: (i // per_b, 0, 0)),
                  pl.BlockSpec((EXPERTS_PER_GROUP, d, D_EXPERT), wmap),
                  pl.BlockSpec((EXPERTS_PER_GROUP, d, D_EXPERT), wmap),
                  pl.BlockSpec((EXPERTS_PER_GROUP, D_EXPERT, d), wmap)],
        out_specs=pl.BlockSpec((tm, d), rowmap),
        scratch_shapes=[pltpu.VMEM((LANES, tm), F32)],
    )
    return pl.pallas_call(
        _moe_kernel,
        grid_spec=grid_spec,
        out_shape=jax.ShapeDtypeStruct((n, d), F32),
        compiler_params=_cparams("arbitrary", "arbitrary"),
        name="moe",
    )(cnt, h2, cmb, x1, gt, wg, wu, wd)


def _pad_cols(w, n):
    return jnp.pad(w, ((0, 0), (0, n - w.shape[1])))


def kernel(x, c, positions, w_ada, b_ada, g_norm_mix, w_in, mu_shift, w0, w_decay_up, a0, w_a_up, w_g_up, k_k, k_a, r_k, ln_x_w, ln_x_b, g_q_a, w_uq, g_kv_a, w_ukv, q_norm, k_norm, w_branch, w_out, g_norm_ffn, w_router_group, b_router_group, w_router_expert, b_router_expert, w_e_gate, w_e_up, w_e_down):
    bsz, seq, d = x.shape
    n = bsz * seq
    row = lambda a: a.reshape(1, -1)

    freqs = ROPE_THETA ** (-(jnp.arange(0, QK_ROPE_DIM, 2, dtype=F32) / QK_ROPE_DIM))
    freq = jnp.zeros((1, LANES), F32).at[0, QK_NOPE_DIM:QK_NOPE_DIM + QK_ROPE_DIM].set(
        jnp.concatenate([freqs, freqs]))
    pos = positions.astype(F32).reshape(bsz, seq, 1)

    x2 = x.reshape(n, d)
    for l in range(w_ada.shape[0]):
        mod = _ada(c, w_ada[l], b_ada[l])
        sh_m, sc_m, gt_m, sh_f, sc_f, gt_f = [m.reshape(bsz, 1, d) for m in jnp.split(mod, 6, axis=-1)]

        w_in_l = w_in[l].astype(BF16)
        wr = w_in_l[:, :RWKV_COLS]
        wm = _pad_cols(w_in_l[:, RWKV_COLS:RWKV_COLS + MLA_COLS], MLA_COLS_PAD)
        wg = w_in_l[:, RWKV_COLS + MLA_COLS:]
        wuq = w_uq[l].reshape(Q_LORA, MLA_HEADS, QK_HEAD_DIM)
        wuq = jnp.pad(wuq, ((0, 0), (0, 0), (0, HEAD_PAD - QK_HEAD_DIM))).reshape(Q_LORA, -1)
        qn = _pad_cols(row(q_norm[l]), LANES)
        kn = _pad_cols(row(k_norm[l]), LANES)
        bound = (QK_HEAD_DIM * jnp.max(jnp.abs(q_norm[l])) * jnp.max(jnp.abs(k_norm[l]))
                 * QK_HEAD_DIM ** -0.5 * LOG2_E)
        fixed = 2.0 * bound <= MAX_FIXED_SHIFT_SPAN
        kshift = jnp.zeros((1, LANES), F32).at[0, QK_HEAD_DIM].set(jnp.where(fixed, -bound, 0.0))
        zr, zm, sg = _inproj(x2, sh_m, sc_m, row(g_norm_mix[l]), wr, wm, wg, seq)
        q, k, v = _mla_prep(zm.reshape(bsz, seq, MLA_COLS_PAD), pos, freq, row(g_q_a[l]),
                            wuq.astype(BF16), row(g_kv_a[l]), w_ukv[l].astype(BF16), qn, kn, kshift)

        zeros_lora = jnp.zeros((DECAY_LORA, D_RWKV), F32)
        wdec = jnp.concatenate([w_decay_up[l], zeros_lora], axis=0)
        wa = jnp.concatenate([zeros_lora, w_a_up[l]], axis=0)
        prep = _rwkv_prep(zr.reshape(bsz, seq, RWKV_COLS), row(mu_shift[l]), row(w0[l]), wdec,
                          row(a0[l]), wa, w_g_up[l], row(k_k[l]), row(k_a[l]), row(r_k[l]))
        y_rwkv = _rwkv_scan(*prep, row(ln_x_w[l]), row(ln_x_b[l]))

        y_mla = _attn(q, k, v, fixed.astype(jnp.int32).reshape(1))

        w_router = _pad_cols(jnp.concatenate([w_router_expert[l], w_router_group[l]], axis=1), LANES)
        b_router = _pad_cols(row(jnp.concatenate([b_router_expert[l], b_router_group[l]])), LANES)
        x1, h2, cmb, cnt = _post(y_rwkv.reshape(n, D_RWKV), y_mla, sg, x2,
                                 gt_m, sh_f, sc_f, row(g_norm_ffn[l]),
                                 w_branch[l, 0].astype(BF16), w_branch[l, 1].astype(BF16),
                                 w_out[l].astype(BF16), w_router, b_router, seq)

        cnt = cnt[:, 0, N_EXPERTS:N_EXPERTS + N_GROUPS].reshape(n // MOE_TILE, -1, N_GROUPS)
        cnt = jnp.sum(cnt, axis=1).astype(jnp.int32).reshape(-1)
        x2 = _moe(cnt, h2, cmb, x1, gt_f, w_e_gate[l].astype(BF16), w_e_up[l].astype(BF16),
                  w_e_down[l].astype(BF16), seq)
    return x2.reshape(bsz, seq, d)
```

```python
import functools

import numpy as np
import jax
import jax.numpy as jnp
from jax import lax
from jax.experimental import pallas as pl
from jax.experimental.pallas import tpu as pltpu

F32 = jnp.float32
BF16 = jnp.bfloat16

D_MODEL = 1024
EPS = 1e-6
RWKV_HEADS = 8
RWKV_HEAD_DIM = 64
D_RWKV = 512
DECAY_LORA = 64
AAA_LORA = 64
GATE_LORA = 128
LN_X_EPS = 64e-5
MLA_HEADS = 8
QK_NOPE_DIM = 64
QK_ROPE_DIM = 32
QK_HEAD_DIM = 96
V_HEAD_DIM = 64
D_MLA = 512
Q_LORA = 256
KV_LORA = 128
ROPE_THETA = 10000.0
RWKV_COLS = 1792
MLA_COLS = 416
MLA_COLS_PAD = 512
N_GROUPS = 4
EXPERTS_PER_GROUP = 8
N_EXPERTS = 32
D_EXPERT = 256

LANES = 128
CHUNK = 64
HEAD_PAD = 128
VT_ROWS = 80
NEG_BIG = -1e30
LOG2_E = 1.4426950408889634
MAX_FIXED_SHIFT_SPAN = 80.0
VMEM_LIMIT = 56 * 1024 * 1024


def _cparams(*sem):
    return pltpu.CompilerParams(dimension_semantics=sem, vmem_limit_bytes=VMEM_LIMIT)


def _dot(a, b):
    return jnp.dot(a, b, preferred_element_type=F32)


def _dot_nt(a, b):
    return lax.dot_general(a, b, (((1,), (1,)), ((), ())), preferred_element_type=F32)


def _dot_tn(a, b):
    return lax.dot_general(a, b, (((0,), (0,)), ((), ())), preferred_element_type=F32)


def _mm(a, b):
    return _dot(a.astype(BF16), b.astype(BF16))


def _split(a):
    hi = a.astype(BF16)
    lo = (a - hi.astype(F32)).astype(BF16)
    return hi, lo


def _mm_rhs_exact(a, b):
    hi, lo = _split(a)
    return _dot(hi, b) + _dot(lo, b)


def _mm_lhs_exact(a, b):
    hi, lo = _split(b)
    return _dot(a, hi) + _dot(a, lo)


def _mm3(a, b):
    ah, al = _split(a)
    bh, bl = _split(b)
    return _dot(ah, bh) + (_dot(ah, bl) + _dot(al, bh))


def _sigmoid(x):
    return 1.0 / (1.0 + jnp.exp(-x))


def _const_spec(shape):
    nd = len(shape)
    return pl.BlockSpec(shape, lambda *_: (0,) * nd)


def _ada_kernel(c_ref, w_ref, b_ref, o_ref):
    c = c_ref[...]
    s = c * _sigmoid(c)
    o_ref[...] = _mm3(s, w_ref[...]) + b_ref[...]


def _ada(c, w, b):
    bsz, d = c.shape
    n = w.shape[1]
    tn = 1024
    return pl.pallas_call(
        _ada_kernel,
        grid=(n // tn,),
        in_specs=[pl.BlockSpec((bsz, d), lambda j: (0, 0)),
                  pl.BlockSpec((d, tn), lambda j: (0, j)),
                  pl.BlockSpec((1, tn), lambda j: (0, j))],
        out_specs=pl.BlockSpec((bsz, tn), lambda j: (0, j)),
        out_shape=jax.ShapeDtypeStruct((bsz, n), F32),
        compiler_params=_cparams("arbitrary"),
        name="ada",
    )(c, w, b.reshape(1, n))


def _rwkv_prep_kernel(z_ref, prev_ref, mu_ref, w0_ref, wdec_ref, a0_ref, wa_ref, wgu_ref,
                      kk_ref, ka_ref, rk_ref, ones_ref, tri_ref, blk_ref,
                      rp_ref, am_ref, bm_ref, km_ref, bh_ref, kh_ref, v_ref,
                      pc_ref, bv_ref, g_ref):
    tt = z_ref.shape[1]
    z = z_ref[0]
    prev = prev_ref[0][7:8, :]
    prev = jnp.where(pl.program_id(1) == 0, 0.0, prev)
    row = lax.broadcasted_iota(jnp.int32, (tt, 1), 0)
    zs = jnp.where(row == 0, prev, pltpu.roll(z, 1, axis=0))
    z = z + (zs - z) * mu_ref[...]
    zr = z[:, 0:D_RWKV]
    zk = z[:, D_RWKV:2 * D_RWKV]
    zv = z[:, 2 * D_RWKV:3 * D_RWKV]
    zwa = z[:, 3 * D_RWKV:3 * D_RWKV + DECAY_LORA + AAA_LORA]
    zg = z[:, 3 * D_RWKV + DECAY_LORA + AAA_LORA:]

    u = -(w0_ref[...] + _mm3(jnp.tanh(zwa), wdec_ref[...]))
    softplus = jnp.maximum(u, 0.0) + jnp.log(1.0 + jnp.exp(-jnp.abs(u)))
    logw = -jnp.exp(-softplus - 0.5)
    a = _sigmoid(a0_ref[...] + _mm(zwa, wa_ref[...]))
    g = _mm(_sigmoid(zg), wgu_ref[...])

    ones_blk = ones_ref[...]
    xk = zk * kk_ref[...]
    ss = _mm_rhs_exact(xk * xk, ones_blk)
    kk = xk * lax.rsqrt(jnp.maximum(ss, 1e-24))
    k = zk * (1.0 + (a - 1.0) * ka_ref[...])
    b = kk * a
    bonus = _mm_rhs_exact(zr * k * rk_ref[...], ones_blk)

    cum = _mm_lhs_exact(tri_ref[...], logw)
    tot = _mm_lhs_exact(blk_ref[...], logw)
    e_neg = jnp.exp(-cum)
    e_tot = jnp.exp(tot)
    bm = b * e_neg
    km = k * e_neg
    rp_ref[0] = (zr * jnp.exp(cum)).astype(BF16)
    am_ref[0] = (-kk * jnp.exp(cum - logw)).astype(BF16)
    bm_ref[0] = bm.astype(BF16)
    km_ref[0] = km.astype(BF16)
    bh_ref[0] = (bm * e_tot).astype(BF16)
    kh_ref[0] = (km * e_tot).astype(BF16)
    v_ref[0] = zv.astype(BF16)
    bv_ref[0] = bonus * zv
    g_ref[0] = g
    nc = tt // CHUNK
    pick = (lax.broadcasted_iota(jnp.int32, (nc, tt), 1)
            == CHUNK * lax.broadcasted_iota(jnp.int32, (nc, tt), 0))
    pc_ref[0] = _mm_lhs_exact(jnp.where(pick, 1.0, 0.0).astype(BF16), e_tot)


def _rwkv_prep(zr3, mu, w0, wdec, a0, wa, wgu, k_k, k_a, r_k):
    bsz, seq, cols = zr3.shape
    tt = 512
    d = D_RWKV
    ones_blk = jnp.asarray(np.kron(np.eye(RWKV_HEADS), np.ones((RWKV_HEAD_DIM, RWKV_HEAD_DIM))), BF16)
    cidx = np.arange(tt) // CHUNK
    same = cidx[:, None] == cidx[None, :]
    tri = jnp.asarray(same & (np.arange(tt)[None, :] <= np.arange(tt)[:, None]), BF16)
    blk = jnp.asarray(same, BF16)
    tok = lambda b, j: (b, j, 0)
    big = pl.BlockSpec((1, tt, d), tok)
    out_bf = jax.ShapeDtypeStruct((bsz, seq, d), BF16)
    out_f = jax.ShapeDtypeStruct((bsz, seq, d), F32)
    return pl.pallas_call(
        _rwkv_prep_kernel,
        grid=(bsz, seq // tt),
        in_specs=[pl.BlockSpec((1, tt, cols), tok),
                  pl.BlockSpec((1, 8, cols), lambda b, j: (b, jnp.maximum(j * (tt // 8) - 1, 0), 0)),
                  _const_spec((1, cols)), _const_spec((1, d)), _const_spec(wdec.shape),
                  _const_spec((1, d)), _const_spec(wa.shape), _const_spec(wgu.shape),
                  _const_spec((1, d)), _const_spec((1, d)), _const_spec((1, d)),
                  _const_spec((d, d)), _const_spec((tt, tt)), _const_spec((tt, tt))],
        out_specs=[big] * 7 + [pl.BlockSpec((1, tt // CHUNK, d), tok), big, big],
        out_shape=[out_bf] * 7 + [jax.ShapeDtypeStruct((bsz, seq // CHUNK, d), F32), out_f, out_f],
        compiler_params=_cparams("arbitrary", "arbitrary"),
        name="rwkv_prep",
    )(zr3, zr3, mu, w0, wdec, a0, wa, wgu, k_k, k_a, r_k, ones_blk, tri, blk)


def _rwkv_scan_kernel(rp_ref, am_ref, bm_ref, km_ref, bh_ref, kh_ref, v_ref, pc_ref,
                      bv_ref, g_ref, lnw_ref, lnb_ref, o_ref, h_scr, y_scr):
    tb = rp_ref.shape[1]
    nc = tb // CHUNK
    pp = rp_ref.shape[2] // LANES
    first_pair = pl.program_id(2) * pp

    @pl.when(pl.program_id(1) == 0)
    def _():
        for p in range(pp):
            h_scr[first_pair + p] = jnp.zeros((LANES, LANES), F32)

    lane = lax.broadcasted_iota(jnp.int32, (1, LANES), 1)
    lane2 = lax.broadcasted_iota(jnp.int32, (1, 2 * LANES), 1)
    in_head = [lane < RWKV_HEAD_DIM, lane >= RWKV_HEAD_DIM]
    in_head2 = [(lane2 % LANES) < RWKV_HEAD_DIM, (lane2 % LANES) >= RWKV_HEAD_DIM]

    def blk(x):
        masks = in_head if x.shape[1] == LANES else in_head2
        zero = jnp.zeros((), x.dtype)
        return jnp.concatenate([jnp.where(masks[0], x, zero), jnp.where(masks[1], x, zero)], axis=0)

    rr = lax.broadcasted_iota(jnp.int32, (CHUNK, LANES), 0)
    cc = lax.broadcasted_iota(jnp.int32, (CHUNK, LANES), 1) % CHUNK
    eye_packed = cc == rr
    r2 = lax.broadcasted_iota(jnp.int32, (2 * CHUNK, LANES), 0)
    c2 = lax.broadcasted_iota(jnp.int32, (2 * CHUNK, LANES), 1) % CHUNK
    causal2 = ((r2 < CHUNK) & (c2 < r2)) | ((r2 >= CHUNK) & (c2 <= r2 - CHUNK))
    level_masks = [((rr >> (k + 1)) == (cc >> (k + 1))) & ((rr >> k) == (cc >> k) + 1)
                   for k in range(6)]
    r128 = lax.broadcasted_iota(jnp.int32, (LANES, LANES), 0)
    c128 = lax.broadcasted_iota(jnp.int32, (LANES, LANES), 1)
    blockdiag = (r128 < RWKV_HEAD_DIM) == (c128 < RWKV_HEAD_DIM)
    eye = r128 == c128

    slabs = [(p, c) for p in range(pp) for c in range(nc)]

    def load(ref, p, c):
        return ref[0, pl.ds(c * CHUNK, CHUNK), pl.ds(p * LANES, LANES)]

    rp = {s: load(rp_ref, *s) for s in slabs}
    am = {s: load(am_ref, *s) for s in slabs}
    v = {s: load(v_ref, *s) for s in slabs}

    a_b, a_k = {}, {}
    for s in slabs:
        lhs = jnp.concatenate([am[s], rp[s]], axis=0)
        a_b[s] = jnp.where(causal2, _dot_nt(lhs, blk(load(bm_ref, *s))), 0.0)
        a_k[s] = jnp.where(causal2, _dot_nt(lhs, blk(load(km_ref, *s))), 0.0)
    akv = {s: _dot(a_k[s].astype(BF16), blk(v[s])) for s in slabs}

    a_ab = {s: a_b[s][:CHUNK] for s in slabs}
    tinv = {s: jnp.where(eye_packed, 1.0, 0.0) + jnp.where(level_masks[0], a_ab[s], 0.0)
            for s in slabs}
    for lm in level_masks[1:]:
        tbf = {s: tinv[s].astype(BF16) for s in slabs}
        inner = {s: _dot(jnp.where(lm, a_ab[s], 0.0).astype(BF16), blk(tbf[s])) for s in slabs}
        tinv = {s: tinv[s] + _dot(tbf[s], blk(inner[s].astype(BF16))) for s in slabs}
    z = {s: _dot(tinv[s].astype(BF16),
                 blk(jnp.concatenate([am[s], akv[s][:CHUNK].astype(BF16)], axis=1)))
         for s in slabs}
    g2 = {s: _dot(a_b[s][CHUNK:].astype(BF16), blk(z[s].astype(BF16))) for s in slabs}

    gy, m, hadd = {}, {}, {}
    for s in slabs:
        p, c = s
        mh = _dot_tn(load(bh_ref, p, c), z[s].astype(BF16))
        kv = _dot_tn(load(kh_ref, p, c), v[s])
        pc = pc_ref[0, c:c + 1, pl.ds(p * LANES, LANES)]
        m[s] = (jnp.where(blockdiag, mh[:, :LANES], 0.0) + jnp.where(eye, pc, 0.0)).astype(BF16)
        hadd[s] = jnp.where(blockdiag, mh[:, LANES:] + kv, 0.0)
        gy[s] = ((rp[s].astype(F32) + g2[s][:, :LANES]).astype(BF16),
                 g2[s][:, LANES:] + akv[s][CHUNK:])

    hs = [h_scr[first_pair + p] for p in range(pp)]
    for c in range(nc):
        for p in range(pp):
            hb = hs[p].astype(BF16)
            gmat, y0 = gy[p, c]
            y_scr[pl.ds(c * CHUNK, CHUNK), pl.ds(p * LANES, LANES)] = _dot(gmat, hb) + y0
            hs[p] = _dot(m[p, c], hb) + hadd[p, c]
    for p in range(pp):
        h_scr[first_pair + p] = hs[p]

    avg = jnp.where(blockdiag, 1.0 / RWKV_HEAD_DIM, 0.0).astype(BF16)
    for p in range(pp):
        cols = pl.ds(p * LANES, LANES)
        y = y_scr[:, cols]
        mean = _mm_rhs_exact(y, avg)
        yc = y - mean
        var = _mm_rhs_exact(yc * yc, avg)
        yn = yc * lax.rsqrt(var + LN_X_EPS)
        out = (yn * lnw_ref[:, cols] + lnb_ref[:, cols] + bv_ref[0, :, cols]) * g_ref[0, :, cols]
        o_ref[0, :, cols] = out.astype(BF16)


def _rwkv_scan(rp, am, bm, km, bh, kh, v, pc, bv, g, lnw, lnb):
    bsz, seq, d = rp.shape
    tb = 512
    npairs = d // LANES
    pp = 4
    width = pp * LANES
    tok = pl.BlockSpec((1, tb, width), lambda b, t, p: (b, t, p))
    vec = pl.BlockSpec((1, width), lambda b, t, p: (0, p))
    return pl.pallas_call(
        _rwkv_scan_kernel,
        grid=(bsz, seq // tb, npairs // pp),
        in_specs=[tok] * 7 + [pl.BlockSpec((1, tb // CHUNK, width), lambda b, t, p: (b, t, p)),
                              tok, tok, vec, vec],
        out_specs=tok,
        out_shape=jax.ShapeDtypeStruct((bsz, seq, d), BF16),
        scratch_shapes=[pltpu.VMEM((npairs, LANES, LANES), F32),
                        pltpu.VMEM((tb, width), F32)],
        compiler_params=_cparams("arbitrary", "arbitrary", "arbitrary"),
        name="rwkv_scan",
    )(rp, am, bm, km, bh, kh, v, pc, bv, g, lnw, lnb)


def _rot_half(a, axis=-1):
    half = QK_ROPE_DIM // 2
    lo = lax.slice_in_dim(a, QK_NOPE_DIM, QK_NOPE_DIM + half, axis=axis)
    hi = lax.slice_in_dim(a, QK_NOPE_DIM + half, QK_NOPE_DIM + QK_ROPE_DIM, axis=axis)
    pads = [(0, 0)] * a.ndim
    pads[axis] = (QK_NOPE_DIM, a.shape[axis] - QK_NOPE_DIM - QK_ROPE_DIM)
    return jnp.pad(jnp.concatenate([-hi, lo], axis=axis), pads)


def _mla_heads(z, pos_ref, freq_ref, gq_ref, wuq_ref, wuqr_ref, gkv_ref, wukv_ref, wvt_ref,
               qn_ref, qnr_ref, kn_ref, knr_ref, kshift_ref, q_ref, k_ref, vt_ref, after_head):
    zq = z[:, :Q_LORA]
    zkv = z[:, Q_LORA:Q_LORA + KV_LORA]
    zpe = z[:, Q_LORA + KV_LORA:]
    ang = pos_ref[0] * freq_ref[...]
    cos = jnp.cos(ang)
    sin = jnp.sin(ang)
    lane = lax.broadcasted_iota(jnp.int32, (1, LANES), 1)
    nope = lane < QK_NOPE_DIM
    half = QK_ROPE_DIM // 2

    qa = (zq * lax.rsqrt(jnp.mean(zq * zq, axis=-1, keepdims=True) + EPS) * gq_ref[...]).astype(BF16)
    q_all = _dot(qa, wuq_ref[...])
    qr_all = _dot(qa, wuqr_ref[...])
    kva = (zkv * lax.rsqrt(jnp.mean(zkv * zkv, axis=-1, keepdims=True) + EPS) * gkv_ref[...]).astype(BF16)
    kv_all = _dot(kva, wukv_ref[...])
    pe = pltpu.roll(zpe, QK_NOPE_DIM, axis=1)
    pe_rot = jnp.where(lane < QK_NOPE_DIM + half,
                       -pltpu.roll(pe, LANES - half, axis=1), pltpu.roll(pe, half, axis=1))
    scale = QK_HEAD_DIM ** -0.5 * LOG2_E
    qc = qn_ref[...] * scale * cos
    qs = qnr_ref[...] * scale * sin
    kc = kn_ref[...] * cos
    ks = knr_ref[...] * sin
    pe_term = pe_rot * ks
    ones_row = lax.broadcasted_iota(jnp.int32, (VT_ROWS, 1), 0) == V_HEAD_DIM
    for h in range(MLA_HEADS):
        cols = slice(h * HEAD_PAD, (h + 1) * HEAD_PAD)
        q = q_all[:, cols]
        r = lax.rsqrt(jnp.sum(q * q, axis=-1, keepdims=True) * (1.0 / QK_HEAD_DIM) + EPS)
        q = r * (q * qc + qr_all[:, cols] * qs)
        q_ref[0, h] = jnp.where(lane == QK_HEAD_DIM, 1.0, q).astype(BF16)
        k = jnp.where(nope, kv_all[:, cols], pe)
        r = lax.rsqrt(jnp.sum(k * k, axis=-1, keepdims=True) * (1.0 / QK_HEAD_DIM) + EPS)
        k_ref[0, h] = (r * (k * kc + pe_term) + kshift_ref[...]).astype(BF16)
        vt = _dot_nt(wvt_ref[h], kva)
        vt_ref[0, h] = jnp.where(ones_row, 1.0, vt).astype(BF16)
        after_head(h)


def _front_kernel(x_ref, sh_ref, sc_ref, g_ref, wr_ref, wm_ref, wg_ref,
                  pos_ref, freq_ref, gq_ref, wuq_ref, wuqr_ref, gkv_ref, wukv_ref, wvt_ref,
                  qn_ref, qnr_ref, kn_ref, knr_ref, kshift_ref,
                  zr_ref, sg_ref, q_ref, k_ref, vt_ref):
    x = x_ref[0]
    ms = jnp.mean(x * x, axis=-1, keepdims=True)
    h = x * lax.rsqrt(ms + EPS) * g_ref[...]
    h = h * (1.0 + sc_ref[0]) + sh_ref[0]
    hb = h.astype(BF16)

    width = 2 * LANES
    slabs = ([(zr_ref, wr_ref, c, False) for c in range(0, zr_ref.shape[2], width)]
             + [(sg_ref, wg_ref, c, True) for c in range(0, sg_ref.shape[2], width)])
    per_head = -(-len(slabs) // MLA_HEADS)

    def project(head):
        for out_ref, w_ref, c, gate in slabs[head * per_head:(head + 1) * per_head]:
            y = _dot(hb, w_ref[:, c:c + width])
            out_ref[0, :, c:c + width] = _sigmoid(y).astype(BF16) if gate else y

    _mla_heads(_dot(hb, wm_ref[...]), pos_ref, freq_ref, gq_ref, wuq_ref, wuqr_ref, gkv_ref,
               wukv_ref, wvt_ref, qn_ref, qnr_ref, kn_ref, knr_ref, kshift_ref, q_ref, k_ref, vt_ref,
               after_head=project)


def _front(x, sh, sc, g, wr, wm, wg, pos, freq, gq, wuq, gkv, wukv, qn, kn, kshift):
    bsz, seq, d = x.shape
    tm = 512
    tok = lambda b, j: (b, j, 0)
    bmap = lambda b, j: (b, 0, 0)
    head = lambda b, j: (b, 0, j, 0)
    once = lambda a: pl.BlockSpec(a.shape, lambda b, j: (0,) * a.ndim, pipeline_mode=pl.Buffered(1))
    wuqr = _rot_half(wuq.reshape(Q_LORA, MLA_HEADS, HEAD_PAD)).reshape(Q_LORA, -1)
    unsign = _rot_half(jnp.ones((1, LANES), F32))
    qnr = _rot_half(qn) * unsign
    knr = _rot_half(kn) * unsign
    wv = wukv.reshape(KV_LORA, MLA_HEADS, HEAD_PAD)[:, :, QK_NOPE_DIM:]
    wvt = jnp.pad(wv.transpose(1, 2, 0), ((0, 0), (0, VT_ROWS - V_HEAD_DIM), (0, 0)))
    vec = _const_spec((1, LANES))
    return pl.pallas_call(
        _front_kernel,
        grid=(bsz, seq // tm),
        in_specs=[pl.BlockSpec((1, tm, d), tok), pl.BlockSpec((1, 1, d), bmap),
                  pl.BlockSpec((1, 1, d), bmap), _const_spec((1, d)), once(wr), once(wm), once(wg),
                  pl.BlockSpec((1, tm, 1), tok), vec, _const_spec((1, Q_LORA)), once(wuq),
                  once(wuqr), _const_spec((1, KV_LORA)), once(wukv), once(wvt),
                  vec, vec, vec, vec, vec],
        out_specs=[pl.BlockSpec((1, tm, wr.shape[1]), tok), pl.BlockSpec((1, tm, wg.shape[1]), tok),
                   pl.BlockSpec((1, MLA_HEADS, tm, HEAD_PAD), head),
                   pl.BlockSpec((1, MLA_HEADS, tm, HEAD_PAD), head),
                   pl.BlockSpec((1, MLA_HEADS, VT_ROWS, tm), lambda b, j: (b, 0, 0, j))],
        out_shape=[jax.ShapeDtypeStruct((bsz, seq, wr.shape[1]), F32),
                   jax.ShapeDtypeStruct((bsz, seq, wg.shape[1]), BF16),
                   jax.ShapeDtypeStruct((bsz, MLA_HEADS, seq, HEAD_PAD), BF16),
                   jax.ShapeDtypeStruct((bsz, MLA_HEADS, seq, HEAD_PAD), BF16),
                   jax.ShapeDtypeStruct((bsz, MLA_HEADS, VT_ROWS, seq), BF16)],
        compiler_params=_cparams("arbitrary", "arbitrary"),
        name="front",
    )(x, sh, sc, g, wr, wm, wg, pos, freq, gq, wuq, wuqr, gkv, wukv, wvt, qn, qnr, kn, knr, kshift)


def _attn_kernel(qi_ref, kj_ref, fixed_ref, q_ref, k_ref, vt_ref, o_ref, m_scr, acc_scr):
    p = pl.program_id(1)
    qi = qi_ref[p]
    kj = kj_ref[p]
    fixed_shift = fixed_ref[0] == 1
    bq = q_ref.shape[2]
    bk = k_ref.shape[2]
    heads = range(MLA_HEADS)

    @pl.when(kj == 0)
    def _():
        m_scr[...] = jnp.full(m_scr.shape, NEG_BIG, F32)
        acc_scr[...] = jnp.zeros(acc_scr.shape, F32)

    def scores(diagonal):
        st = [_dot_nt(k_ref[0, h], q_ref[0, h]) for h in heads]
        if diagonal:
            visible = (lax.broadcasted_iota(jnp.int32, (bk, bq), 0)
                       <= lax.broadcasted_iota(jnp.int32, (bk, bq), 1))
            st = [jnp.where(visible, x, NEG_BIG) for x in st]
        return st

    def block_fixed(diagonal):
        st = scores(diagonal)
        pr = [jnp.exp2(st[h]).astype(BF16) for h in heads]
        pv = [_dot(vt_ref[0, h], pr[h]) for h in heads]
        for h in heads:
            acc_scr[h] += pv[h]

    def block(diagonal):
        st = scores(diagonal)
        m_prev = [m_scr[h] for h in heads]
        m_new = [jnp.maximum(m_prev[h], jnp.max(st[h], axis=0, keepdims=True)) for h in heads]
        pr = [jnp.exp2(st[h] - m_new[h]).astype(BF16) for h in heads]
        alpha = [jnp.exp2(m_prev[h] - m_new[h]) for h in heads]
        pv = [_dot(vt_ref[0, h], pr[h]) for h in heads]
        for h in heads:
            acc_scr[h] = alpha[h] * acc_scr[h] + pv[h]
            m_scr[h] = m_new[h]

    for diagonal in (False, True):
        on_diag = (kj == qi) if diagonal else (kj != qi)
        pl.when(on_diag & fixed_shift)(functools.partial(block_fixed, diagonal))
        pl.when(on_diag & jnp.logical_not(fixed_shift))(functools.partial(block, diagonal))

    @pl.when(kj == qi)
    def _():
        for h in heads:
            acc = acc_scr[h]
            o = acc[:V_HEAD_DIM] * (1.0 / acc[V_HEAD_DIM:V_HEAD_DIM + 1])
            o_ref[0, h * V_HEAD_DIM:(h + 1) * V_HEAD_DIM, :] = o.astype(BF16)


def _attn(q, k, v, fixed):
    bsz, nh, seq, _ = q.shape
    bq = bk = 512
    nq = seq // bq
    qi = np.concatenate([np.full(i + 1, i) for i in range(nq)]).astype(np.int32)
    kj = np.concatenate([np.arange(i + 1) for i in range(nq)]).astype(np.int32)
    grid_spec = pltpu.PrefetchScalarGridSpec(
        num_scalar_prefetch=3,
        grid=(bsz, len(qi)),
        in_specs=[pl.BlockSpec((1, nh, bq, HEAD_PAD), lambda b, p, qi, kj, f: (b, 0, qi[p], 0)),
                  pl.BlockSpec((1, nh, bk, HEAD_PAD), lambda b, p, qi, kj, f: (b, 0, kj[p], 0)),
                  pl.BlockSpec((1, nh, VT_ROWS, bk), lambda b, p, qi, kj, f: (b, 0, 0, kj[p]))],
        out_specs=pl.BlockSpec((1, nh * V_HEAD_DIM, bq), lambda b, p, qi, kj, f: (b, 0, qi[p])),
        scratch_shapes=[pltpu.VMEM((nh, 1, bq), F32), pltpu.VMEM((nh, VT_ROWS, bq), F32)],
    )
    return pl.pallas_call(
        _attn_kernel,
        grid_spec=grid_spec,
        out_shape=jax.ShapeDtypeStruct((bsz, nh * V_HEAD_DIM, seq), BF16),
        compiler_params=_cparams("arbitrary", "arbitrary"),
        name="mla_attn",
    )(jnp.asarray(qi), jnp.asarray(kj), fixed, q, k, v)


def _first_index_of_max(vals, lane):
    mx = jnp.max(vals, axis=-1, keepdims=True)
    idx = jnp.min(jnp.where(vals == mx, lane, LANES), axis=-1, keepdims=True)
    return mx, idx


def _post_kernel(yr_ref, ymt_ref, sg_ref, x_ref, gt_ref, sh_ref, sc_ref, g_ref,
                 wb0_ref, wb1_ref, wo_ref, wr_ref, br_ref, x1_ref, h2_ref, cmb_ref, cnt_ref):
    sg = sg_ref[...].astype(F32)
    merged = (sg[:, :D_MODEL] * _dot(yr_ref[...], wb0_ref[...])
              + sg[:, D_MODEL:] * _dot_tn(ymt_ref[0], wb1_ref[...]))
    x1 = x_ref[...] + gt_ref[0] * _mm(merged, wo_ref[...])
    x1_ref[...] = x1
    ms = jnp.mean(x1 * x1, axis=-1, keepdims=True)
    h2 = x1 * lax.rsqrt(ms + EPS) * g_ref[...]
    h2 = h2 * (1.0 + sc_ref[0]) + sh_ref[0]
    h2_ref[...] = h2.astype(BF16)

    logits = _mm3(h2, wr_ref[...]) + br_ref[...]
    lane = lax.broadcasted_iota(jnp.int32, logits.shape, 1).astype(F32)
    gl = jnp.where((lane >= N_EXPERTS) & (lane < N_EXPERTS + N_GROUPS), logits, NEG_BIG)
    gmax, gidx = _first_index_of_max(gl, lane)
    p_g = 1.0 / jnp.sum(jnp.exp(gl - gmax), axis=-1, keepdims=True)
    first = (gidx - N_EXPERTS) * EXPERTS_PER_GROUP
    el = jnp.where((lane >= first) & (lane < first + EXPERTS_PER_GROUP), logits, NEG_BIG)
    m1, i1 = _first_index_of_max(el, lane)
    z = jnp.sum(jnp.exp(el - m1), axis=-1, keepdims=True)
    el2 = jnp.where(lane == i1, NEG_BIG, el)
    m2, i2 = _first_index_of_max(el2, lane)
    p1 = 1.0 / z
    p2 = jnp.exp(m2 - m1) / z
    tot = p1 + p2
    group_onehot = jnp.where(lane == gidx, 1.0, 0.0)
    cmb_ref[...] = (jnp.where(lane == i1, p1 / tot * p_g, 0.0)
                    + jnp.where(lane == i2, p2 / tot * p_g, 0.0) + group_onehot)
    cnt_ref[0] = jnp.broadcast_to(jnp.sum(group_onehot, axis=0, keepdims=True), cnt_ref.shape[1:])


def _post(yr, ym, sg, x2, gt, sh, sc, g, wb0, wb1, wo, wr, br, seq):
    n, d = x2.shape
    tm = 512
    per_b = seq // tm
    bmap = lambda i: (i // per_b, 0, 0)
    rowmap = lambda i: (i, 0)
    return pl.pallas_call(
        _post_kernel,
        grid=(n // tm,),
        in_specs=[pl.BlockSpec((tm, D_RWKV), rowmap),
                  pl.BlockSpec((1, D_MLA, tm), lambda i: (i // per_b, 0, i % per_b)),
                  pl.BlockSpec((tm, 2 * d), rowmap), pl.BlockSpec((tm, d), rowmap),
                  pl.BlockSpec((1, 1, d), bmap), pl.BlockSpec((1, 1, d), bmap),
                  pl.BlockSpec((1, 1, d), bmap), _const_spec((1, d)),
                  _const_spec(wb0.shape), _const_spec(wb1.shape), _const_spec(wo.shape),
                  _const_spec(wr.shape), _const_spec(br.shape)],
        out_specs=[pl.BlockSpec((tm, d), rowmap), pl.BlockSpec((tm, d), rowmap),
                   pl.BlockSpec((tm, LANES), rowmap),
                   pl.BlockSpec((1, 8, LANES), lambda i: (i, 0, 0))],
        out_shape=[jax.ShapeDtypeStruct((n, d), F32), jax.ShapeDtypeStruct((n, d), BF16),
                   jax.ShapeDtypeStruct((n, LANES), F32),
                   jax.ShapeDtypeStruct((n // tm, 8, LANES), F32)],
        compiler_params=_cparams("arbitrary"),
        name="post",
    )(yr, ym, sg, x2, gt, sh, sc, g, wb0, wb1, wo, wr, br)


MOE_TILE = 1024
MOE_CHUNK = 256


def _moe_kernel(cnt_ref, h_ref, cmb_ref, x1_ref, gt_ref, wg_ref, wu_ref, wd_ref, o_ref,
                key_scr):
    i = pl.program_id(0)
    g = pl.program_id(1)
    tm = h_ref.shape[0]
    lane = lax.broadcasted_iota(jnp.int32, (1, LANES), 1)

    @pl.when(g == 0)
    def _():
        o_ref[...] = x1_ref[...]
        onehot = jnp.where((lane >= N_EXPERTS) & (lane < N_EXPERTS + N_GROUPS), cmb_ref[...], 0.0)
        before = (lax.broadcasted_iota(jnp.int32, (tm, tm), 1)
                  < lax.broadcasted_iota(jnp.int32, (tm, tm), 0))
        rank = _dot(jnp.where(before, 1.0, 0.0).astype(BF16), onehot.astype(BF16))
        key = jnp.where(onehot > 0.5, rank, -1.0)
        key_scr[...] = key.T

    key_row = key_scr[pl.ds(N_EXPERTS + g, 1), :]
    h = h_ref[...]
    cmb = cmb_ref[...]
    cmb_hi, cmb_lo = _split(cmb)
    gt = gt_ref[0]
    first_lane = g * EXPERTS_PER_GROUP

    def chunk(c, carry):
        slot = (c * MOE_CHUNK + lax.broadcasted_iota(jnp.int32, (MOE_CHUNK, 1), 0)).astype(F32)
        sel = jnp.where(key_row == slot, 1.0, 0.0).astype(BF16)
        xg = _dot(sel, h).astype(BF16)
        cw = _dot(sel, cmb_hi) + _dot(sel, cmb_lo)
        y = jnp.zeros((MOE_CHUNK, o_ref.shape[1]), F32)
        for e in range(EXPERTS_PER_GROUP):
            gate = _dot(xg, wg_ref[e])
            hid = gate * _sigmoid(gate) * _dot(xg, wu_ref[e])
            cwe = jnp.sum(jnp.where(lane == first_lane + e, cw, 0.0), axis=-1, keepdims=True)
            y = y + _mm(hid * cwe, wd_ref[e])
        o_ref[...] += _dot_tn(sel, (y * gt).astype(BF16))
        return carry

    n_chunks = (cnt_ref[i * N_GROUPS + g] + (MOE_CHUNK - 1)) // MOE_CHUNK
    lax.fori_loop(0, n_chunks, chunk, 0)


def _moe(cnt, h2, cmb, x1, gt, wg, wu, wd, seq):
    n, d = x1.shape
    tm = MOE_TILE
    per_b = seq // tm
    rowmap = lambda i, g, cnt: (i, 0)
    wmap = lambda i, g, cnt: (g, 0, 0)
    grid_spec = pltpu.PrefetchScalarGridSpec(
        num_scalar_prefetch=1,
        grid=(n // tm, N_GROUPS),
        in_specs=[pl.BlockSpec((tm, d), rowmap), pl.BlockSpec((tm, LANES), rowmap),
                  pl.BlockSpec((tm, d), rowmap),
                  pl.BlockSpec((1, 1, d), lambda i, g, cnt: (i // per_b, 0, 0)),
                  pl.BlockSpec((EXPERTS_PER_GROUP, d, D_EXPERT), wmap),
                  pl.BlockSpec((EXPERTS_PER_GROUP, d, D_EXPERT), wmap),
                  pl.BlockSpec((EXPERTS_PER_GROUP, D_EXPERT, d), wmap)],
        out_specs=pl.BlockSpec((tm, d), rowmap),
        scratch_shapes=[pltpu.VMEM((LANES, tm), F32)],
    )
    return pl.pallas_call(
        _moe_kernel,
        grid_spec=grid_spec,
        out_shape=jax.ShapeDtypeStruct((n, d), F32),
        compiler_params=_cparams("arbitrary", "arbitrary"),
        name="moe",
    )(cnt, h2, cmb, x1, gt, wg, wu, wd)


def _pad_cols(w, n):
    return jnp.pad(w, ((0, 0), (0, n - w.shape[1])))


def kernel(x, c, positions, w_ada, b_ada, g_norm_mix, w_in, mu_shift, w0, w_decay_up, a0, w_a_up, w_g_up, k_k, k_a, r_k, ln_x_w, ln_x_b, g_q_a, w_uq, g_kv_a, w_ukv, q_norm, k_norm, w_branch, w_out, g_norm_ffn, w_router_group, b_router_group, w_router_expert, b_router_expert, w_e_gate, w_e_up, w_e_down):
    bsz, seq, d = x.shape
    n = bsz * seq
    row = lambda a: a.reshape(1, -1)

    freqs = ROPE_THETA ** (-(jnp.arange(0, QK_ROPE_DIM, 2, dtype=F32) / QK_ROPE_DIM))
    freq = jnp.zeros((1, LANES), F32).at[0, QK_NOPE_DIM:QK_NOPE_DIM + QK_ROPE_DIM].set(
        jnp.concatenate([freqs, freqs]))
    pos = positions.astype(F32).reshape(bsz, seq, 1)

    x2 = x.reshape(n, d)
    for l in range(w_ada.shape[0]):
        mod = _ada(c, w_ada[l], b_ada[l])
        sh_m, sc_m, gt_m, sh_f, sc_f, gt_f = [m.reshape(bsz, 1, d) for m in jnp.split(mod, 6, axis=-1)]

        w_in_l = w_in[l].astype(BF16)
        wr = w_in_l[:, :RWKV_COLS]
        wm = _pad_cols(w_in_l[:, RWKV_COLS:RWKV_COLS + MLA_COLS], MLA_COLS_PAD)
        wg = w_in_l[:, RWKV_COLS + MLA_COLS:]
        wuq = w_uq[l].reshape(Q_LORA, MLA_HEADS, QK_HEAD_DIM)
        wuq = jnp.pad(wuq, ((0, 0), (0, 0), (0, HEAD_PAD - QK_HEAD_DIM))).reshape(Q_LORA, -1)
        qn = _pad_cols(row(q_norm[l]), LANES)
        kn = _pad_cols(row(k_norm[l]), LANES)
        bound = (QK_HEAD_DIM * jnp.max(jnp.abs(q_norm[l])) * jnp.max(jnp.abs(k_norm[l]))
                 * QK_HEAD_DIM ** -0.5 * LOG2_E)
        fixed = 2.0 * bound <= MAX_FIXED_SHIFT_SPAN
        kshift = jnp.zeros((1, LANES), F32).at[0, QK_HEAD_DIM].set(jnp.where(fixed, -bound, 0.0))
        zr, sg, q, k, v = _front(x2.reshape(bsz, seq, d), sh_m, sc_m, row(g_norm_mix[l]), wr, wm, wg,
                                 pos, freq, row(g_q_a[l]), wuq.astype(BF16), row(g_kv_a[l]),
                                 w_ukv[l].astype(BF16), qn, kn, kshift)
        sg = sg.reshape(n, -1)

        zeros_lora = jnp.zeros((DECAY_LORA, D_RWKV), F32)
        wdec = jnp.concatenate([w_decay_up[l], zeros_lora], axis=0)
        wa = jnp.concatenate([zeros_lora, w_a_up[l]], axis=0)
        prep = _rwkv_prep(zr.reshape(bsz, seq, RWKV_COLS), row(mu_shift[l]), row(w0[l]), wdec,
                          row(a0[l]), wa, w_g_up[l], row(k_k[l]), row(k_a[l]), row(r_k[l]))
        y_rwkv = _rwkv_scan(*prep, row(ln_x_w[l]), row(ln_x_b[l]))

        y_mla = _attn(q, k, v, fixed.astype(jnp.int32).reshape(1))

        w_router = _pad_cols(jnp.concatenate([w_router_expert[l], w_router_group[l]], axis=1), LANES)
        b_router = _pad_cols(row(jnp.concatenate([b_router_expert[l], b_router_group[l]])), LANES)
        x1, h2, cmb, cnt = _post(y_rwkv.reshape(n, D_RWKV), y_mla, sg, x2,
                                 gt_m, sh_f, sc_f, row(g_norm_ffn[l]),
                                 w_branch[l, 0].astype(BF16), w_branch[l, 1].astype(BF16),
                                 w_out[l].astype(BF16), w_router, b_router, seq)

        cnt = cnt[:, 0, N_EXPERTS:N_EXPERTS + N_GROUPS].reshape(n // MOE_TILE, -1, N_GROUPS)
        cnt = jnp.sum(cnt, axis=1).astype(jnp.int32).reshape(-1)
        x2 = _moe(cnt, h2, cmb, x1, gt_f, w_e_gate[l].astype(BF16), w_e_up[l].astype(BF16),
                  w_e_down[l].astype(BF16), seq)
    return x2.reshape(bsz, seq, d)
```

```python
import functools

import numpy as np
import jax
import jax.numpy as jnp
from jax import lax
from jax.experimental import pallas as pl
from jax.experimental.pallas import tpu as pltpu

F32 = jnp.float32
BF16 = jnp.bfloat16

D_MODEL = 1024
EPS = 1e-6
RWKV_HEADS = 8
RWKV_HEAD_DIM = 64
D_RWKV = 512
DECAY_LORA = 64
AAA_LORA = 64
GATE_LORA = 128
LN_X_EPS = 64e-5
MLA_HEADS = 8
QK_NOPE_DIM = 64
QK_ROPE_DIM = 32
QK_HEAD_DIM = 96
V_HEAD_DIM = 64
D_MLA = 512
Q_LORA = 256
KV_LORA = 128
ROPE_THETA = 10000.0
RWKV_COLS = 1792
MLA_COLS = 416
MLA_COLS_PAD = 512
N_GROUPS = 4
EXPERTS_PER_GROUP = 8
N_EXPERTS = 32
D_EXPERT = 256

LANES = 128
CHUNK = 64
HEAD_PAD = 128
VT_ROWS = 128
NEG_BIG = -1e30
LOG2_E = 1.4426950408889634
MAX_FIXED_SHIFT_SPAN = 80.0
VMEM_LIMIT = 56 * 1024 * 1024


def _cparams(*sem):
    return pltpu.CompilerParams(dimension_semantics=sem, vmem_limit_bytes=VMEM_LIMIT)


def _dot(a, b):
    return jnp.dot(a, b, preferred_element_type=F32)


def _dot_nt(a, b):
    return lax.dot_general(a, b, (((1,), (1,)), ((), ())), preferred_element_type=F32)


def _dot_tn(a, b):
    return lax.dot_general(a, b, (((0,), (0,)), ((), ())), preferred_element_type=F32)


def _mm(a, b):
    return _dot(a.astype(BF16), b.astype(BF16))


def _split(a):
    hi = a.astype(BF16)
    lo = (a - hi.astype(F32)).astype(BF16)
    return hi, lo


def _mm_rhs_exact(a, b):
    hi, lo = _split(a)
    return _dot(hi, b) + _dot(lo, b)


def _mm_lhs_exact(a, b):
    hi, lo = _split(b)
    return _dot(a, hi) + _dot(a, lo)


def _mm3(a, b):
    ah, al = _split(a)
    bh, bl = _split(b)
    return _dot(ah, bh) + (_dot(ah, bl) + _dot(al, bh))


def _sigmoid(x):
    return 1.0 / (1.0 + jnp.exp(-x))


def _const_spec(shape):
    nd = len(shape)
    return pl.BlockSpec(shape, lambda *_: (0,) * nd)


def _ada_kernel(c_ref, w_ref, b_ref, o_ref):
    c = c_ref[...]
    s = c * _sigmoid(c)
    o_ref[...] = _mm3(s, w_ref[...]) + b_ref[...]


def _ada(c, w, b):
    bsz, d = c.shape
    n = w.shape[1]
    tn = 1024
    return pl.pallas_call(
        _ada_kernel,
        grid=(n // tn,),
        in_specs=[pl.BlockSpec((bsz, d), lambda j: (0, 0)),
                  pl.BlockSpec((d, tn), lambda j: (0, j)),
                  pl.BlockSpec((1, tn), lambda j: (0, j))],
        out_specs=pl.BlockSpec((bsz, tn), lambda j: (0, j)),
        out_shape=jax.ShapeDtypeStruct((bsz, n), F32),
        compiler_params=_cparams("arbitrary"),
        name="ada",
    )(c, w, b.reshape(1, n))


def _rwkv_prep_kernel(z_ref, prev_ref, mu_ref, w0_ref, wdec_ref, a0_ref, wa_ref, wgu_ref,
                      kk_ref, ka_ref, rk_ref, ones_ref, tri_ref, blk_ref,
                      rp_ref, am_ref, bm_ref, km_ref, bh_ref, kh_ref, v_ref,
                      pc_ref, bv_ref, g_ref):
    tt = z_ref.shape[1]
    z = z_ref[0]
    prev = prev_ref[0][7:8, :]
    prev = jnp.where(pl.program_id(1) == 0, 0.0, prev)
    row = lax.broadcasted_iota(jnp.int32, (tt, 1), 0)
    zs = jnp.where(row == 0, prev, pltpu.roll(z, 1, axis=0))
    z = z + (zs - z) * mu_ref[...]
    zr = z[:, 0:D_RWKV]
    zk = z[:, D_RWKV:2 * D_RWKV]
    zv = z[:, 2 * D_RWKV:3 * D_RWKV]
    zwa = z[:, 3 * D_RWKV:3 * D_RWKV + DECAY_LORA + AAA_LORA]
    zg = z[:, 3 * D_RWKV + DECAY_LORA + AAA_LORA:]

    u = -(w0_ref[...] + _mm3(jnp.tanh(zwa), wdec_ref[...]))
    softplus = jnp.maximum(u, 0.0) + jnp.log(1.0 + jnp.exp(-jnp.abs(u)))
    logw = -jnp.exp(-softplus - 0.5)
    a = _sigmoid(a0_ref[...] + _mm(zwa, wa_ref[...]))
    g = _mm(_sigmoid(zg), wgu_ref[...])

    ones_blk = ones_ref[...]
    xk = zk * kk_ref[...]
    ss = _mm_rhs_exact(xk * xk, ones_blk)
    kk = xk * lax.rsqrt(jnp.maximum(ss, 1e-24))
    k = zk * (1.0 + (a - 1.0) * ka_ref[...])
    b = kk * a
    bonus = _mm_rhs_exact(zr * k * rk_ref[...], ones_blk)

    cum = _mm_lhs_exact(tri_ref[...], logw)
    tot = _mm_lhs_exact(blk_ref[...], logw)
    e_neg = jnp.exp(-cum)
    e_tot = jnp.exp(tot)
    bm = b * e_neg
    km = k * e_neg
    rp_ref[0] = (zr * jnp.exp(cum)).astype(BF16)
    am_ref[0] = (-kk * jnp.exp(cum - logw)).astype(BF16)
    bm_ref[0] = bm.astype(BF16)
    km_ref[0] = km.astype(BF16)
    bh_ref[0] = (bm * e_tot).astype(BF16)
    kh_ref[0] = (km * e_tot).astype(BF16)
    v_ref[0] = zv.astype(BF16)
    bv_ref[0] = bonus * zv
    g_ref[0] = g
    nc = tt // CHUNK
    pick = (lax.broadcasted_iota(jnp.int32, (nc, tt), 1)
            == CHUNK * lax.broadcasted_iota(jnp.int32, (nc, tt), 0))
    pc_ref[0] = _mm_lhs_exact(jnp.where(pick, 1.0, 0.0).astype(BF16), e_tot)


def _rwkv_prep(zr3, mu, w0, wdec, a0, wa, wgu, k_k, k_a, r_k):
    bsz, seq, cols = zr3.shape
    tt = 512
    d = D_RWKV
    ones_blk = jnp.asarray(np.kron(np.eye(RWKV_HEADS), np.ones((RWKV_HEAD_DIM, RWKV_HEAD_DIM))), BF16)
    cidx = np.arange(tt) // CHUNK
    same = cidx[:, None] == cidx[None, :]
    tri = jnp.asarray(same & (np.arange(tt)[None, :] <= np.arange(tt)[:, None]), BF16)
    blk = jnp.asarray(same, BF16)
    tok = lambda b, j: (b, j, 0)
    big = pl.BlockSpec((1, tt, d), tok)
    out_bf = jax.ShapeDtypeStruct((bsz, seq, d), BF16)
    out_f = jax.ShapeDtypeStruct((bsz, seq, d), F32)
    return pl.pallas_call(
        _rwkv_prep_kernel,
        grid=(bsz, seq // tt),
        in_specs=[pl.BlockSpec((1, tt, cols), tok),
                  pl.BlockSpec((1, 8, cols), lambda b, j: (b, jnp.maximum(j * (tt // 8) - 1, 0), 0)),
                  _const_spec((1, cols)), _const_spec((1, d)), _const_spec(wdec.shape),
                  _const_spec((1, d)), _const_spec(wa.shape), _const_spec(wgu.shape),
                  _const_spec((1, d)), _const_spec((1, d)), _const_spec((1, d)),
                  _const_spec((d, d)), _const_spec((tt, tt)), _const_spec((tt, tt))],
        out_specs=[big] * 7 + [pl.BlockSpec((1, tt // CHUNK, d), tok), big, big],
        out_shape=[out_bf] * 7 + [jax.ShapeDtypeStruct((bsz, seq // CHUNK, d), F32), out_f, out_f],
        compiler_params=_cparams("arbitrary", "arbitrary"),
        name="rwkv_prep",
    )(zr3, zr3, mu, w0, wdec, a0, wa, wgu, k_k, k_a, r_k, ones_blk, tri, blk)


def _rwkv_scan_kernel(rp_ref, am_ref, bm_ref, km_ref, bh_ref, kh_ref, v_ref, pc_ref,
                      bv_ref, g_ref, lnw_ref, lnb_ref, o_ref, h_scr, y_scr):
    tb = rp_ref.shape[1]
    nc = tb // CHUNK
    pp = rp_ref.shape[2] // LANES
    first_pair = pl.program_id(2) * pp

    @pl.when(pl.program_id(1) == 0)
    def _():
        for p in range(pp):
            h_scr[first_pair + p] = jnp.zeros((LANES, LANES), F32)

    lane = lax.broadcasted_iota(jnp.int32, (1, LANES), 1)
    lane2 = lax.broadcasted_iota(jnp.int32, (1, 2 * LANES), 1)
    in_head = [lane < RWKV_HEAD_DIM, lane >= RWKV_HEAD_DIM]
    in_head2 = [(lane2 % LANES) < RWKV_HEAD_DIM, (lane2 % LANES) >= RWKV_HEAD_DIM]

    def blk(x):
        masks = in_head if x.shape[1] == LANES else in_head2
        zero = jnp.zeros((), x.dtype)
        return jnp.concatenate([jnp.where(masks[0], x, zero), jnp.where(masks[1], x, zero)], axis=0)

    rr = lax.broadcasted_iota(jnp.int32, (CHUNK, LANES), 0)
    cc = lax.broadcasted_iota(jnp.int32, (CHUNK, LANES), 1) % CHUNK
    eye_packed = cc == rr
    r2 = lax.broadcasted_iota(jnp.int32, (2 * CHUNK, LANES), 0)
    c2 = lax.broadcasted_iota(jnp.int32, (2 * CHUNK, LANES), 1) % CHUNK
    causal2 = ((r2 < CHUNK) & (c2 < r2)) | ((r2 >= CHUNK) & (c2 <= r2 - CHUNK))
    level_masks = [((rr >> (k + 1)) == (cc >> (k + 1))) & ((rr >> k) == (cc >> k) + 1)
                   for k in range(6)]
    r128 = lax.broadcasted_iota(jnp.int32, (LANES, LANES), 0)
    c128 = lax.broadcasted_iota(jnp.int32, (LANES, LANES), 1)
    blockdiag = (r128 < RWKV_HEAD_DIM) == (c128 < RWKV_HEAD_DIM)
    eye = r128 == c128

    slabs = [(p, c) for p in range(pp) for c in range(nc)]

    def load(ref, p, c):
        return ref[0, pl.ds(c * CHUNK, CHUNK), pl.ds(p * LANES, LANES)]

    rp = {s: load(rp_ref, *s) for s in slabs}
    am = {s: load(am_ref, *s) for s in slabs}
    v = {s: load(v_ref, *s) for s in slabs}

    a_b, a_k = {}, {}
    for s in slabs:
        lhs = jnp.concatenate([am[s], rp[s]], axis=0)
        a_b[s] = jnp.where(causal2, _dot_nt(lhs, blk(load(bm_ref, *s))), 0.0)
        a_k[s] = jnp.where(causal2, _dot_nt(lhs, blk(load(km_ref, *s))), 0.0)
    akv = {s: _dot(a_k[s].astype(BF16), blk(v[s])) for s in slabs}

    a_ab = {s: a_b[s][:CHUNK] for s in slabs}
    tinv = {s: jnp.where(eye_packed, 1.0, 0.0) + jnp.where(level_masks[0], a_ab[s], 0.0)
            for s in slabs}
    for lm in level_masks[1:]:
        tbf = {s: tinv[s].astype(BF16) for s in slabs}
        inner = {s: _dot(jnp.where(lm, a_ab[s], 0.0).astype(BF16), blk(tbf[s])) for s in slabs}
        tinv = {s: tinv[s] + _dot(tbf[s], blk(inner[s].astype(BF16))) for s in slabs}
    z = {s: _dot(tinv[s].astype(BF16),
                 blk(jnp.concatenate([am[s], akv[s][:CHUNK].astype(BF16)], axis=1)))
         for s in slabs}
    g2 = {s: _dot(a_b[s][CHUNK:].astype(BF16), blk(z[s].astype(BF16))) for s in slabs}

    gy, m, hadd = {}, {}, {}
    for s in slabs:
        p, c = s
        mh = _dot_tn(load(bh_ref, p, c), z[s].astype(BF16))
        kv = _dot_tn(load(kh_ref, p, c), v[s])
        pc = pc_ref[0, c:c + 1, pl.ds(p * LANES, LANES)]
        m[s] = (jnp.where(blockdiag, mh[:, :LANES], 0.0) + jnp.where(eye, pc, 0.0)).astype(BF16)
        hadd[s] = jnp.where(blockdiag, mh[:, LANES:] + kv, 0.0)
        gy[s] = ((rp[s].astype(F32) + g2[s][:, :LANES]).astype(BF16),
                 g2[s][:, LANES:] + akv[s][CHUNK:])

    hs = [h_scr[first_pair + p] for p in range(pp)]
    for c in range(nc):
        for p in range(pp):
            hb = hs[p].astype(BF16)
            gmat, y0 = gy[p, c]
            y_scr[pl.ds(c * CHUNK, CHUNK), pl.ds(p * LANES, LANES)] = _dot(gmat, hb) + y0
            hs[p] = _dot(m[p, c], hb) + hadd[p, c]
    for p in range(pp):
        h_scr[first_pair + p] = hs[p]

    avg = jnp.where(blockdiag, 1.0 / RWKV_HEAD_DIM, 0.0).astype(BF16)
    for p in range(pp):
        cols = pl.ds(p * LANES, LANES)
        y = y_scr[:, cols]
        mean = _mm_rhs_exact(y, avg)
        yc = y - mean
        var = _mm_rhs_exact(yc * yc, avg)
        yn = yc * lax.rsqrt(var + LN_X_EPS)
        out = (yn * lnw_ref[:, cols] + lnb_ref[:, cols] + bv_ref[0, :, cols]) * g_ref[0, :, cols]
        o_ref[0, :, cols] = out.astype(BF16)


def _rwkv_scan(rp, am, bm, km, bh, kh, v, pc, bv, g, lnw, lnb):
    bsz, seq, d = rp.shape
    tb = 512
    npairs = d // LANES
    pp = 4
    width = pp * LANES
    tok = pl.BlockSpec((1, tb, width), lambda b, t, p: (b, t, p))
    vec = pl.BlockSpec((1, width), lambda b, t, p: (0, p))
    return pl.pallas_call(
        _rwkv_scan_kernel,
        grid=(bsz, seq // tb, npairs // pp),
        in_specs=[tok] * 7 + [pl.BlockSpec((1, tb // CHUNK, width), lambda b, t, p: (b, t, p)),
                              tok, tok, vec, vec],
        out_specs=tok,
        out_shape=jax.ShapeDtypeStruct((bsz, seq, d), BF16),
        scratch_shapes=[pltpu.VMEM((npairs, LANES, LANES), F32),
                        pltpu.VMEM((tb, width), F32)],
        compiler_params=_cparams("arbitrary", "arbitrary", "arbitrary"),
        name="rwkv_scan",
    )(rp, am, bm, km, bh, kh, v, pc, bv, g, lnw, lnb)


def _rot_half(a, axis=-1):
    half = QK_ROPE_DIM // 2
    lo = lax.slice_in_dim(a, QK_NOPE_DIM, QK_NOPE_DIM + half, axis=axis)
    hi = lax.slice_in_dim(a, QK_NOPE_DIM + half, QK_NOPE_DIM + QK_ROPE_DIM, axis=axis)
    pads = [(0, 0)] * a.ndim
    pads[axis] = (QK_NOPE_DIM, a.shape[axis] - QK_NOPE_DIM - QK_ROPE_DIM)
    return jnp.pad(jnp.concatenate([-hi, lo], axis=axis), pads)


def _mla_heads(z, pos_ref, freq_ref, gq_ref, wuq_ref, wuqr_ref, gkv_ref, wukv_ref, wvt_ref,
               qn_ref, qnr_ref, kn_ref, knr_ref, kshift_ref, q_ref, k_ref, vt_ref, after_head):
    zq = z[:, :Q_LORA]
    zkv = z[:, Q_LORA:Q_LORA + KV_LORA]
    zpe = z[:, Q_LORA + KV_LORA:]
    ang = pos_ref[0] * freq_ref[...]
    cos = jnp.cos(ang)
    sin = jnp.sin(ang)
    lane = lax.broadcasted_iota(jnp.int32, (1, LANES), 1)
    nope = lane < QK_NOPE_DIM
    half = QK_ROPE_DIM // 2

    qa = (zq * lax.rsqrt(jnp.mean(zq * zq, axis=-1, keepdims=True) + EPS) * gq_ref[...]).astype(BF16)
    q_all = _dot(qa, wuq_ref[...])
    qr_all = _dot(qa, wuqr_ref[...])
    kva = (zkv * lax.rsqrt(jnp.mean(zkv * zkv, axis=-1, keepdims=True) + EPS) * gkv_ref[...]).astype(BF16)
    kv_all = _dot(kva, wukv_ref[...])
    pe = pltpu.roll(zpe, QK_NOPE_DIM, axis=1)
    pe_rot = jnp.where(lane < QK_NOPE_DIM + half,
                       -pltpu.roll(pe, LANES - half, axis=1), pltpu.roll(pe, half, axis=1))
    scale = QK_HEAD_DIM ** -0.5 * LOG2_E
    qc = qn_ref[...] * scale * cos
    qs = qnr_ref[...] * scale * sin
    kc = kn_ref[...] * cos
    ks = knr_ref[...] * sin
    pe_term = pe_rot * ks
    ones_row = lax.broadcasted_iota(jnp.int32, (VT_ROWS, 1), 0) == V_HEAD_DIM
    for h in range(MLA_HEADS):
        cols = slice(h * HEAD_PAD, (h + 1) * HEAD_PAD)
        q = q_all[:, cols]
        r = lax.rsqrt(jnp.sum(q * q, axis=-1, keepdims=True) * (1.0 / QK_HEAD_DIM) + EPS)
        q = r * (q * qc + qr_all[:, cols] * qs)
        q_ref[0, h] = jnp.where(lane == QK_HEAD_DIM, 1.0, q).astype(BF16)
        k = jnp.where(nope, kv_all[:, cols], pe)
        r = lax.rsqrt(jnp.sum(k * k, axis=-1, keepdims=True) * (1.0 / QK_HEAD_DIM) + EPS)
        k_ref[0, h] = (r * (k * kc + pe_term) + kshift_ref[...]).astype(BF16)
        vt = _dot_nt(wvt_ref[h], kva)
        vt_ref[0, h] = jnp.where(ones_row, 1.0, vt).astype(BF16)
        after_head(h)


def _front_kernel(x_ref, sh_ref, sc_ref, g_ref, wr_ref, wm_ref, wg_ref,
                  pos_ref, freq_ref, gq_ref, wuq_ref, wuqr_ref, gkv_ref, wukv_ref, wvt_ref,
                  qn_ref, qnr_ref, kn_ref, knr_ref, kshift_ref,
                  zr_ref, sg_ref, q_ref, k_ref, vt_ref):
    x = x_ref[0]
    ms = jnp.mean(x * x, axis=-1, keepdims=True)
    h = x * lax.rsqrt(ms + EPS) * g_ref[...]
    h = h * (1.0 + sc_ref[0]) + sh_ref[0]
    hb = h.astype(BF16)

    width = 2 * LANES
    slabs = ([(zr_ref, wr_ref, c, False) for c in range(0, zr_ref.shape[2], width)]
             + [(sg_ref, wg_ref, c, True) for c in range(0, sg_ref.shape[2], width)])
    per_head = -(-len(slabs) // (MLA_HEADS + 1))

    def project(stage):
        for out_ref, w_ref, c, gate in slabs[stage * per_head:(stage + 1) * per_head]:
            y = _dot(hb, w_ref[:, c:c + width])
            out_ref[0, :, c:c + width] = _sigmoid(y).astype(BF16) if gate else y

    zm = _dot(hb, wm_ref[...])
    project(0)
    _mla_heads(zm, pos_ref, freq_ref, gq_ref, wuq_ref, wuqr_ref, gkv_ref,
               wukv_ref, wvt_ref, qn_ref, qnr_ref, kn_ref, knr_ref, kshift_ref, q_ref, k_ref, vt_ref,
               after_head=lambda h: project(h + 1))


def _front(x, sh, sc, g, wr, wm, wg, pos, freq, gq, wuq, gkv, wukv, qn, kn, kshift):
    bsz, seq, d = x.shape
    tm = 512
    tok = lambda b, j: (b, j, 0)
    bmap = lambda b, j: (b, 0, 0)
    head = lambda b, j: (b, 0, j, 0)
    once = lambda a: pl.BlockSpec(a.shape, lambda b, j: (0,) * a.ndim, pipeline_mode=pl.Buffered(1))
    wuqr = _rot_half(wuq.reshape(Q_LORA, MLA_HEADS, HEAD_PAD)).reshape(Q_LORA, -1)
    unsign = _rot_half(jnp.ones((1, LANES), F32))
    qnr = _rot_half(qn) * unsign
    knr = _rot_half(kn) * unsign
    wv = wukv.reshape(KV_LORA, MLA_HEADS, HEAD_PAD)[:, :, QK_NOPE_DIM:]
    wvt = jnp.pad(wv.transpose(1, 2, 0), ((0, 0), (0, VT_ROWS - V_HEAD_DIM), (0, 0)))
    vec = _const_spec((1, LANES))
    return pl.pallas_call(
        _front_kernel,
        grid=(bsz, seq // tm),
        in_specs=[pl.BlockSpec((1, tm, d), tok), pl.BlockSpec((1, 1, d), bmap),
                  pl.BlockSpec((1, 1, d), bmap), _const_spec((1, d)), once(wr), once(wm), once(wg),
                  pl.BlockSpec((1, tm, 1), tok), vec, _const_spec((1, Q_LORA)), once(wuq),
                  once(wuqr), _const_spec((1, KV_LORA)), once(wukv), once(wvt),
                  vec, vec, vec, vec, vec],
        out_specs=[pl.BlockSpec((1, tm, wr.shape[1]), tok), pl.BlockSpec((1, tm, wg.shape[1]), tok),
                   pl.BlockSpec((1, MLA_HEADS, tm, HEAD_PAD), head),
                   pl.BlockSpec((1, MLA_HEADS, tm, HEAD_PAD), head),
                   pl.BlockSpec((1, MLA_HEADS, VT_ROWS, tm), lambda b, j: (b, 0, 0, j))],
        out_shape=[jax.ShapeDtypeStruct((bsz, seq, wr.shape[1]), F32),
                   jax.ShapeDtypeStruct((bsz, seq, wg.shape[1]), BF16),
                   jax.ShapeDtypeStruct((bsz, MLA_HEADS, seq, HEAD_PAD), BF16),
                   jax.ShapeDtypeStruct((bsz, MLA_HEADS, seq, HEAD_PAD), BF16),
                   jax.ShapeDtypeStruct((bsz, MLA_HEADS, VT_ROWS, seq), BF16)],
        compiler_params=_cparams("arbitrary", "arbitrary"),
        name="front",
    )(x, sh, sc, g, wr, wm, wg, pos, freq, gq, wuq, wuqr, gkv, wukv, wvt, qn, qnr, kn, knr, kshift)


def _attn_kernel(qi_ref, kj_ref, fixed_ref, q_ref, k_ref, vt_ref, o_ref, m_scr, acc_scr):
    p = pl.program_id(1)
    qi = qi_ref[p]
    kj = kj_ref[p]
    fixed_shift = fixed_ref[0] == 1
    bq = q_ref.shape[2]
    bk = k_ref.shape[2]
    heads = range(MLA_HEADS)

    @pl.when(kj == 0)
    def _():
        m_scr[...] = jnp.full(m_scr.shape, NEG_BIG, F32)
        acc_scr[...] = jnp.zeros(acc_scr.shape, F32)

    def scores(diagonal):
        st = [_dot_nt(k_ref[0, h], q_ref[0, h]) for h in heads]
        if diagonal:
            visible = (lax.broadcasted_iota(jnp.int32, (bk, bq), 0)
                       <= lax.broadcasted_iota(jnp.int32, (bk, bq), 1))
            st = [jnp.where(visible, x, NEG_BIG) for x in st]
        return st

    def block_fixed(diagonal):
        st = scores(diagonal)
        pr = [jnp.exp2(st[h]).astype(BF16) for h in heads]
        pv = [_dot(vt_ref[0, h], pr[h]) for h in heads]
        for h in heads:
            acc_scr[h] += pv[h]

    def block(diagonal):
        st = scores(diagonal)
        m_prev = [m_scr[h] for h in heads]
        m_new = [jnp.maximum(m_prev[h], jnp.max(st[h], axis=0, keepdims=True)) for h in heads]
        pr = [jnp.exp2(st[h] - m_new[h]).astype(BF16) for h in heads]
        alpha = [jnp.exp2(m_prev[h] - m_new[h]) for h in heads]
        pv = [_dot(vt_ref[0, h], pr[h]) for h in heads]
        for h in heads:
            acc_scr[h] = alpha[h] * acc_scr[h] + pv[h]
            m_scr[h] = m_new[h]

    for diagonal in (False, True):
        on_diag = (kj == qi) if diagonal else (kj != qi)
        pl.when(on_diag & fixed_shift)(functools.partial(block_fixed, diagonal))
        pl.when(on_diag & jnp.logical_not(fixed_shift))(functools.partial(block, diagonal))

    @pl.when(kj == qi)
    def _():
        for h in heads:
            acc = acc_scr[h]
            o = acc[:V_HEAD_DIM] * (1.0 / acc[V_HEAD_DIM:V_HEAD_DIM + 1])
            o_ref[0, h * V_HEAD_DIM:(h + 1) * V_HEAD_DIM, :] = o.astype(BF16)


def _attn(q, k, v, fixed):
    bsz, nh, seq, _ = q.shape
    bq = bk = 512
    nq = seq // bq
    qi = np.concatenate([np.full(i + 1, i) for i in range(nq)]).astype(np.int32)
    kj = np.concatenate([np.arange(i + 1) for i in range(nq)]).astype(np.int32)
    grid_spec = pltpu.PrefetchScalarGridSpec(
        num_scalar_prefetch=3,
        grid=(bsz, len(qi)),
        in_specs=[pl.BlockSpec((1, nh, bq, HEAD_PAD), lambda b, p, qi, kj, f: (b, 0, qi[p], 0)),
                  pl.BlockSpec((1, nh, bk, HEAD_PAD), lambda b, p, qi, kj, f: (b, 0, kj[p], 0)),
                  pl.BlockSpec((1, nh, VT_ROWS, bk), lambda b, p, qi, kj, f: (b, 0, 0, kj[p]))],
        out_specs=pl.BlockSpec((1, nh * V_HEAD_DIM, bq), lambda b, p, qi, kj, f: (b, 0, qi[p])),
        scratch_shapes=[pltpu.VMEM((nh, 1, bq), F32), pltpu.VMEM((nh, VT_ROWS, bq), F32)],
    )
    return pl.pallas_call(
        _attn_kernel,
        grid_spec=grid_spec,
        out_shape=jax.ShapeDtypeStruct((bsz, nh * V_HEAD_DIM, seq), BF16),
        compiler_params=_cparams("arbitrary", "arbitrary"),
        name="mla_attn",
    )(jnp.asarray(qi), jnp.asarray(kj), fixed, q, k, v)


def _first_index_of_max(vals, lane):
    mx = jnp.max(vals, axis=-1, keepdims=True)
    idx = jnp.min(jnp.where(vals == mx, lane, LANES), axis=-1, keepdims=True)
    return mx, idx


def _post_kernel(yr_ref, ymt_ref, sg_ref, x_ref, gt_ref, sh_ref, sc_ref, g_ref,
                 wb0_ref, wb1_ref, wo_ref, wr_ref, br_ref, x1_ref, h2_ref, cmb_ref, cnt_ref):
    sg = sg_ref[...].astype(F32)
    merged = (sg[:, :D_MODEL] * _dot(yr_ref[...], wb0_ref[...])
              + sg[:, D_MODEL:] * _dot_tn(ymt_ref[0], wb1_ref[...]))
    x1 = x_ref[...] + gt_ref[0] * _mm(merged, wo_ref[...])
    x1_ref[...] = x1
    ms = jnp.mean(x1 * x1, axis=-1, keepdims=True)
    h2 = x1 * lax.rsqrt(ms + EPS) * g_ref[...]
    h2 = h2 * (1.0 + sc_ref[0]) + sh_ref[0]
    h2_ref[...] = h2.astype(BF16)

    logits = _mm3(h2, wr_ref[...]) + br_ref[...]
    lane = lax.broadcasted_iota(jnp.int32, logits.shape, 1).astype(F32)
    gl = jnp.where((lane >= N_EXPERTS) & (lane < N_EXPERTS + N_GROUPS), logits, NEG_BIG)
    gmax, gidx = _first_index_of_max(gl, lane)
    p_g = 1.0 / jnp.sum(jnp.exp(gl - gmax), axis=-1, keepdims=True)
    first = (gidx - N_EXPERTS) * EXPERTS_PER_GROUP
    el = jnp.where((lane >= first) & (lane < first + EXPERTS_PER_GROUP), logits, NEG_BIG)
    m1, i1 = _first_index_of_max(el, lane)
    z = jnp.sum(jnp.exp(el - m1), axis=-1, keepdims=True)
    el2 = jnp.where(lane == i1, NEG_BIG, el)
    m2, i2 = _first_index_of_max(el2, lane)
    p1 = 1.0 / z
    p2 = jnp.exp(m2 - m1) / z
    tot = p1 + p2
    group_onehot = jnp.where(lane == gidx, 1.0, 0.0)
    cmb_ref[...] = (jnp.where(lane == i1, p1 / tot * p_g, 0.0)
                    + jnp.where(lane == i2, p2 / tot * p_g, 0.0) + group_onehot)
    cnt_ref[0] = jnp.broadcast_to(jnp.sum(group_onehot, axis=0, keepdims=True), cnt_ref.shape[1:])


def _post(yr, ym, sg, x2, gt, sh, sc, g, wb0, wb1, wo, wr, br, seq):
    n, d = x2.shape
    tm = 512
    per_b = seq // tm
    bmap = lambda i: (i // per_b, 0, 0)
    rowmap = lambda i: (i, 0)
    return pl.pallas_call(
        _post_kernel,
        grid=(n // tm,),
        in_specs=[pl.BlockSpec((tm, D_RWKV), rowmap),
                  pl.BlockSpec((1, D_MLA, tm), lambda i: (i // per_b, 0, i % per_b)),
                  pl.BlockSpec((tm, 2 * d), rowmap), pl.BlockSpec((tm, d), rowmap),
                  pl.BlockSpec((1, 1, d), bmap), pl.BlockSpec((1, 1, d), bmap),
                  pl.BlockSpec((1, 1, d), bmap), _const_spec((1, d)),
                  _const_spec(wb0.shape), _const_spec(wb1.shape), _const_spec(wo.shape),
                  _const_spec(wr.shape), _const_spec(br.shape)],
        out_specs=[pl.BlockSpec((tm, d), rowmap), pl.BlockSpec((tm, d), rowmap),
                   pl.BlockSpec((tm, LANES), rowmap),
                   pl.BlockSpec((1, 8, LANES), lambda i: (i, 0, 0))],
        out_shape=[jax.ShapeDtypeStruct((n, d), F32), jax.ShapeDtypeStruct((n, d), BF16),
                   jax.ShapeDtypeStruct((n, LANES), F32),
                   jax.ShapeDtypeStruct((n // tm, 8, LANES), F32)],
        compiler_params=_cparams("arbitrary"),
        name="post",
    )(yr, ym, sg, x2, gt, sh, sc, g, wb0, wb1, wo, wr, br)


MOE_TILE = 1024
MOE_CHUNK = 256


def _moe_kernel(cnt_ref, h_ref, cmb_ref, x1_ref, gt_ref, wg_ref, wu_ref, wd_ref, o_ref,
                key_scr):
    i = pl.program_id(0)
    g = pl.program_id(1)
    tm = h_ref.shape[0]
    lane = lax.broadcasted_iota(jnp.int32, (1, LANES), 1)

    @pl.when(g == 0)
    def _():
        o_ref[...] = x1_ref[...]
        onehot = jnp.where((lane >= N_EXPERTS) & (lane < N_EXPERTS + N_GROUPS), cmb_ref[...], 0.0)
        before = (lax.broadcasted_iota(jnp.int32, (tm, tm), 1)
                  < lax.broadcasted_iota(jnp.int32, (tm, tm), 0))
        rank = _dot(jnp.where(before, 1.0, 0.0).astype(BF16), onehot.astype(BF16))
        key = jnp.where(onehot > 0.5, rank, -1.0)
        key_scr[...] = key.T

    key_row = key_scr[pl.ds(N_EXPERTS + g, 1), :]
    h = h_ref[...]
    cmb = cmb_ref[...]
    cmb_hi, cmb_lo = _split(cmb)
    gt = gt_ref[0]
    first_lane = g * EXPERTS_PER_GROUP

    def chunk(c, carry):
        slot = (c * MOE_CHUNK + lax.broadcasted_iota(jnp.int32, (MOE_CHUNK, 1), 0)).astype(F32)
        sel = jnp.where(key_row == slot, 1.0, 0.0).astype(BF16)
        xg = _dot(sel, h).astype(BF16)
        cw = _dot(sel, cmb_hi) + _dot(sel, cmb_lo)
        y = jnp.zeros((MOE_CHUNK, o_ref.shape[1]), F32)
        for e in range(EXPERTS_PER_GROUP):
            gate = _dot(xg, wg_ref[e])
            hid = gate * _sigmoid(gate) * _dot(xg, wu_ref[e])
            cwe = jnp.sum(jnp.where(lane == first_lane + e, cw, 0.0), axis=-1, keepdims=True)
            y = y + _mm(hid * cwe, wd_ref[e])
        o_ref[...] += _dot_tn(sel, (y * gt).astype(BF16))
        return carry

    n_chunks = (cnt_ref[i * N_GROUPS + g] + (MOE_CHUNK - 1)) // MOE_CHUNK
    lax.fori_loop(0, n_chunks, chunk, 0)


def _moe(cnt, h2, cmb, x1, gt, wg, wu, wd, seq):
    n, d = x1.shape
    tm = MOE_TILE
    per_b = seq // tm
    rowmap = lambda i, g, cnt: (i, 0)
    wmap = lambda i, g, cnt: (g, 0, 0)
    grid_spec = pltpu.PrefetchScalarGridSpec(
        num_scalar_prefetch=1,
        grid=(n // tm, N_GROUPS),
        in_specs=[pl.BlockSpec((tm, d), rowmap), pl.BlockSpec((tm, LANES), rowmap),
                  pl.BlockSpec((tm, d), rowmap),
                  pl.BlockSpec((1, 1, d), lambda i, g, cnt: (i // per_b, 0, 0)),
                  pl.BlockSpec((EXPERTS_PER_GROUP, d, D_EXPERT), wmap),
                  pl.BlockSpec((EXPERTS_PER_GROUP, d, D_EXPERT), wmap),
                  pl.BlockSpec((EXPERTS_PER_GROUP, D_EXPERT, d), wmap)],
        out_specs=pl.BlockSpec((tm, d), rowmap),
        scratch_shapes=[pltpu.VMEM((LANES, tm), F32)],
    )
    return pl.pallas_call(
        _moe_kernel,
        grid_spec=grid_spec,
        out_shape=jax.ShapeDtypeStruct((n, d), F32),
        compiler_params=_cparams("arbitrary", "arbitrary"),
        name="moe",
    )(cnt, h2, cmb, x1, gt, wg, wu, wd)


def _pad_cols(w, n):
    return jnp.pad(w, ((0, 0), (0, n - w.shape[1])))


def kernel(x, c, positions, w_ada, b_ada, g_norm_mix, w_in, mu_shift, w0, w_decay_up, a0, w_a_up, w_g_up, k_k, k_a, r_k, ln_x_w, ln_x_b, g_q_a, w_uq, g_kv_a, w_ukv, q_norm, k_norm, w_branch, w_out, g_norm_ffn, w_router_group, b_router_group, w_router_expert, b_router_expert, w_e_gate, w_e_up, w_e_down):
    bsz, seq, d = x.shape
    n = bsz * seq
    row = lambda a: a.reshape(1, -1)

    freqs = ROPE_THETA ** (-(jnp.arange(0, QK_ROPE_DIM, 2, dtype=F32) / QK_ROPE_DIM))
    freq = jnp.zeros((1, LANES), F32).at[0, QK_NOPE_DIM:QK_NOPE_DIM + QK_ROPE_DIM].set(
        jnp.concatenate([freqs, freqs]))
    pos = positions.astype(F32).reshape(bsz, seq, 1)

    x2 = x.reshape(n, d)
    for l in range(w_ada.shape[0]):
        mod = _ada(c, w_ada[l], b_ada[l])
        sh_m, sc_m, gt_m, sh_f, sc_f, gt_f = [m.reshape(bsz, 1, d) for m in jnp.split(mod, 6, axis=-1)]

        w_in_l = w_in[l].astype(BF16)
        wr = w_in_l[:, :RWKV_COLS]
        wm = _pad_cols(w_in_l[:, RWKV_COLS:RWKV_COLS + MLA_COLS], MLA_COLS_PAD)
        wg = w_in_l[:, RWKV_COLS + MLA_COLS:]
        wuq = w_uq[l].reshape(Q_LORA, MLA_HEADS, QK_HEAD_DIM)
        wuq = jnp.pad(wuq, ((0, 0), (0, 0), (0, HEAD_PAD - QK_HEAD_DIM))).reshape(Q_LORA, -1)
        qn = _pad_cols(row(q_norm[l]), LANES)
        kn = _pad_cols(row(k_norm[l]), LANES)
        bound = (QK_HEAD_DIM * jnp.max(jnp.abs(q_norm[l])) * jnp.max(jnp.abs(k_norm[l]))
                 * QK_HEAD_DIM ** -0.5 * LOG2_E)
        fixed = 2.0 * bound <= MAX_FIXED_SHIFT_SPAN
        kshift = jnp.zeros((1, LANES), F32).at[0, QK_HEAD_DIM].set(jnp.where(fixed, -bound, 0.0))
        zr, sg, q, k, v = _front(x2.reshape(bsz, seq, d), sh_m, sc_m, row(g_norm_mix[l]), wr, wm, wg,
                                 pos, freq, row(g_q_a[l]), wuq.astype(BF16), row(g_kv_a[l]),
                                 w_ukv[l].astype(BF16), qn, kn, kshift)
        sg = sg.reshape(n, -1)

        zeros_lora = jnp.zeros((DECAY_LORA, D_RWKV), F32)
        wdec = jnp.concatenate([w_decay_up[l], zeros_lora], axis=0)
        wa = jnp.concatenate([zeros_lora, w_a_up[l]], axis=0)
        prep = _rwkv_prep(zr.reshape(bsz, seq, RWKV_COLS), row(mu_shift[l]), row(w0[l]), wdec,
                          row(a0[l]), wa, w_g_up[l], row(k_k[l]), row(k_a[l]), row(r_k[l]))
        y_rwkv = _rwkv_scan(*prep, row(ln_x_w[l]), row(ln_x_b[l]))

        y_mla = _attn(q, k, v, fixed.astype(jnp.int32).reshape(1))

        w_router = _pad_cols(jnp.concatenate([w_router_expert[l], w_router_group[l]], axis=1), LANES)
        b_router = _pad_cols(row(jnp.concatenate([b_router_expert[l], b_router_group[l]])), LANES)
        x1, h2, cmb, cnt = _post(y_rwkv.reshape(n, D_RWKV), y_mla, sg, x2,
                                 gt_m, sh_f, sc_f, row(g_norm_ffn[l]),
                                 w_branch[l, 0].astype(BF16), w_branch[l, 1].astype(BF16),
                                 w_out[l].astype(BF16), w_router, b_router, seq)

        cnt = cnt[:, 0, N_EXPERTS:N_EXPERTS + N_GROUPS].reshape(n // MOE_TILE, -1, N_GROUPS)
        cnt = jnp.sum(cnt, axis=1).astype(jnp.int32).reshape(-1)
        x2 = _moe(cnt, h2, cmb, x1, gt_f, w_e_gate[l].astype(BF16), w_e_up[l].astype(BF16),
                  w_e_down[l].astype(BF16), seq)
    return x2.reshape(bsz, seq, d)
```

```python
import functools

import numpy as np
import jax
import jax.numpy as jnp
from jax import lax
from jax.experimental import pallas as pl
from jax.experimental.pallas import tpu as pltpu

F32 = jnp.float32
BF16 = jnp.bfloat16

D_MODEL = 1024
EPS = 1e-6
RWKV_HEADS = 8
RWKV_HEAD_DIM = 64
D_RWKV = 512
DECAY_LORA = 64
AAA_LORA = 64
GATE_LORA = 128
LN_X_EPS = 64e-5
MLA_HEADS = 8
QK_NOPE_DIM = 64
QK_ROPE_DIM = 32
QK_HEAD_DIM = 96
V_HEAD_DIM = 64
D_MLA = 512
Q_LORA = 256
KV_LORA = 128
ROPE_THETA = 10000.0
RWKV_COLS = 1792
MLA_COLS = 416
MLA_COLS_PAD = 512
N_GROUPS = 4
EXPERTS_PER_GROUP = 8
N_EXPERTS = 32
D_EXPERT = 256

LANES = 128
CHUNK = 64
HEAD_PAD = 128
VT_ROWS = 128
NEG_BIG = -1e30
LOG2_E = 1.4426950408889634
MAX_FIXED_SHIFT_SPAN = 80.0
VMEM_LIMIT = 56 * 1024 * 1024


def _cparams(*sem):
    return pltpu.CompilerParams(dimension_semantics=sem, vmem_limit_bytes=VMEM_LIMIT)


def _dot(a, b):
    return jnp.dot(a, b, preferred_element_type=F32)


def _dot_nt(a, b):
    return lax.dot_general(a, b, (((1,), (1,)), ((), ())), preferred_element_type=F32)


def _dot_tn(a, b):
    return lax.dot_general(a, b, (((0,), (0,)), ((), ())), preferred_element_type=F32)


def _mm(a, b):
    return _dot(a.astype(BF16), b.astype(BF16))


def _split(a):
    hi = a.astype(BF16)
    lo = (a - hi.astype(F32)).astype(BF16)
    return hi, lo


def _mm_rhs_exact(a, b):
    hi, lo = _split(a)
    return _dot(hi, b) + _dot(lo, b)


def _mm_lhs_exact(a, b):
    hi, lo = _split(b)
    return _dot(a, hi) + _dot(a, lo)


def _mm3(a, b):
    ah, al = _split(a)
    bh, bl = _split(b)
    return _dot(ah, bh) + (_dot(ah, bl) + _dot(al, bh))


def _sigmoid(x):
    return 1.0 / (1.0 + jnp.exp(-x))


def _const_spec(shape):
    nd = len(shape)
    return pl.BlockSpec(shape, lambda *_: (0,) * nd)


def _ada_kernel(c_ref, w_ref, b_ref, o_ref):
    c = c_ref[...]
    s = c * _sigmoid(c)
    o_ref[...] = _mm3(s, w_ref[...]) + b_ref[...]


def _ada(c, w, b):
    bsz, d = c.shape
    n = w.shape[1]
    tn = 1024
    return pl.pallas_call(
        _ada_kernel,
        grid=(n // tn,),
        in_specs=[pl.BlockSpec((bsz, d), lambda j: (0, 0)),
                  pl.BlockSpec((d, tn), lambda j: (0, j)),
                  pl.BlockSpec((1, tn), lambda j: (0, j))],
        out_specs=pl.BlockSpec((bsz, tn), lambda j: (0, j)),
        out_shape=jax.ShapeDtypeStruct((bsz, n), F32),
        compiler_params=_cparams("arbitrary"),
        name="ada",
    )(c, w, b.reshape(1, n))


def _rwkv_prep_kernel(z_ref, prev_ref, mu_ref, w0_ref, wdec_ref, a0_ref, wa_ref, wgu_ref,
                      kk_ref, ka_ref, rk_ref, ones_ref, tri_ref, blk_ref,
                      rp_ref, am_ref, bm_ref, km_ref, bh_ref, kh_ref, v_ref,
                      pc_ref, bv_ref, g_ref):
    tt = z_ref.shape[1]
    z = z_ref[0]
    prev = prev_ref[0][7:8, :]
    prev = jnp.where(pl.program_id(1) == 0, 0.0, prev)
    row = lax.broadcasted_iota(jnp.int32, (tt, 1), 0)
    zs = jnp.where(row == 0, prev, pltpu.roll(z, 1, axis=0))
    z = z + (zs - z) * mu_ref[...]
    zr = z[:, 0:D_RWKV]
    zk = z[:, D_RWKV:2 * D_RWKV]
    zv = z[:, 2 * D_RWKV:3 * D_RWKV]
    zwa = z[:, 3 * D_RWKV:3 * D_RWKV + DECAY_LORA + AAA_LORA]
    zg = z[:, 3 * D_RWKV + DECAY_LORA + AAA_LORA:]

    u = -(w0_ref[...] + _mm3(jnp.tanh(zwa), wdec_ref[...]))
    softplus = jnp.maximum(u, 0.0) + jnp.log(1.0 + jnp.exp(-jnp.abs(u)))
    logw = -jnp.exp(-softplus - 0.5)
    a = _sigmoid(a0_ref[...] + _mm(zwa, wa_ref[...]))
    g = _mm(_sigmoid(zg), wgu_ref[...])

    ones_blk = ones_ref[...]
    xk = zk * kk_ref[...]
    ss = _mm_rhs_exact(xk * xk, ones_blk)
    kk = xk * lax.rsqrt(jnp.maximum(ss, 1e-24))
    k = zk * (1.0 + (a - 1.0) * ka_ref[...])
    b = kk * a
    bonus = _mm_rhs_exact(zr * k * rk_ref[...], ones_blk)

    cum = _mm_lhs_exact(tri_ref[...], logw)
    tot = _mm_lhs_exact(blk_ref[...], logw)
    e_neg = jnp.exp(-cum)
    e_tot = jnp.exp(tot)
    bm = b * e_neg
    km = k * e_neg
    rp_ref[0] = (zr * jnp.exp(cum)).astype(BF16)
    am_ref[0] = (-kk * jnp.exp(cum - logw)).astype(BF16)
    bm_ref[0] = bm.astype(BF16)
    km_ref[0] = km.astype(BF16)
    bh_ref[0] = (bm * e_tot).astype(BF16)
    kh_ref[0] = (km * e_tot).astype(BF16)
    v_ref[0] = zv.astype(BF16)
    bv_ref[0] = bonus * zv
    g_ref[0] = g
    nc = tt // CHUNK
    pick = (lax.broadcasted_iota(jnp.int32, (nc, tt), 1)
            == CHUNK * lax.broadcasted_iota(jnp.int32, (nc, tt), 0))
    pc_ref[0] = _mm_lhs_exact(jnp.where(pick, 1.0, 0.0).astype(BF16), e_tot)


def _rwkv_prep(zr3, mu, w0, wdec, a0, wa, wgu, k_k, k_a, r_k):
    bsz, seq, cols = zr3.shape
    tt = 512
    d = D_RWKV
    ones_blk = jnp.asarray(np.kron(np.eye(RWKV_HEADS), np.ones((RWKV_HEAD_DIM, RWKV_HEAD_DIM))), BF16)
    cidx = np.arange(tt) // CHUNK
    same = cidx[:, None] == cidx[None, :]
    tri = jnp.asarray(same & (np.arange(tt)[None, :] <= np.arange(tt)[:, None]), BF16)
    blk = jnp.asarray(same, BF16)
    tok = lambda b, j: (b, j, 0)
    big = pl.BlockSpec((1, tt, d), tok)
    out_bf = jax.ShapeDtypeStruct((bsz, seq, d), BF16)
    out_f = jax.ShapeDtypeStruct((bsz, seq, d), F32)
    return pl.pallas_call(
        _rwkv_prep_kernel,
        grid=(bsz, seq // tt),
        in_specs=[pl.BlockSpec((1, tt, cols), tok),
                  pl.BlockSpec((1, 8, cols), lambda b, j: (b, jnp.maximum(j * (tt // 8) - 1, 0), 0)),
                  _const_spec((1, cols)), _const_spec((1, d)), _const_spec(wdec.shape),
                  _const_spec((1, d)), _const_spec(wa.shape), _const_spec(wgu.shape),
                  _const_spec((1, d)), _const_spec((1, d)), _const_spec((1, d)),
                  _const_spec((d, d)), _const_spec((tt, tt)), _const_spec((tt, tt))],
        out_specs=[big] * 7 + [pl.BlockSpec((1, tt // CHUNK, d), tok), big, big],
        out_shape=[out_bf] * 7 + [jax.ShapeDtypeStruct((bsz, seq // CHUNK, d), F32), out_f, out_f],
        compiler_params=_cparams("arbitrary", "arbitrary"),
        name="rwkv_prep",
    )(zr3, zr3, mu, w0, wdec, a0, wa, wgu, k_k, k_a, r_k, ones_blk, tri, blk)


def _rwkv_scan_kernel(rp_ref, am_ref, bm_ref, km_ref, bh_ref, kh_ref, v_ref, pc_ref,
                      bv_ref, g_ref, lnw_ref, lnb_ref, o_ref, h_scr, y_scr):
    tb = rp_ref.shape[1]
    nc = tb // CHUNK
    pp = rp_ref.shape[2] // LANES
    first_pair = pl.program_id(2) * pp

    @pl.when(pl.program_id(1) == 0)
    def _():
        for p in range(pp):
            h_scr[first_pair + p] = jnp.zeros((LANES, LANES), F32)

    lane = lax.broadcasted_iota(jnp.int32, (1, LANES), 1)
    lane2 = lax.broadcasted_iota(jnp.int32, (1, 2 * LANES), 1)
    in_head = [lane < RWKV_HEAD_DIM, lane >= RWKV_HEAD_DIM]
    in_head2 = [(lane2 % LANES) < RWKV_HEAD_DIM, (lane2 % LANES) >= RWKV_HEAD_DIM]

    def blk(x):
        masks = in_head if x.shape[1] == LANES else in_head2
        zero = jnp.zeros((), x.dtype)
        return jnp.concatenate([jnp.where(masks[0], x, zero), jnp.where(masks[1], x, zero)], axis=0)

    rr = lax.broadcasted_iota(jnp.int32, (CHUNK, LANES), 0)
    cc = lax.broadcasted_iota(jnp.int32, (CHUNK, LANES), 1) % CHUNK
    eye_packed = cc == rr
    r2 = lax.broadcasted_iota(jnp.int32, (2 * CHUNK, LANES), 0)
    c2 = lax.broadcasted_iota(jnp.int32, (2 * CHUNK, LANES), 1) % CHUNK
    causal2 = ((r2 < CHUNK) & (c2 < r2)) | ((r2 >= CHUNK) & (c2 <= r2 - CHUNK))
    level_masks = [((rr >> (k + 1)) == (cc >> (k + 1))) & ((rr >> k) == (cc >> k) + 1)
                   for k in range(6)]
    r128 = lax.broadcasted_iota(jnp.int32, (LANES, LANES), 0)
    c128 = lax.broadcasted_iota(jnp.int32, (LANES, LANES), 1)
    blockdiag = (r128 < RWKV_HEAD_DIM) == (c128 < RWKV_HEAD_DIM)
    eye = r128 == c128

    slabs = [(p, c) for p in range(pp) for c in range(nc)]

    def load(ref, p, c):
        return ref[0, pl.ds(c * CHUNK, CHUNK), pl.ds(p * LANES, LANES)]

    rp = {s: load(rp_ref, *s) for s in slabs}
    am = {s: load(am_ref, *s) for s in slabs}
    v = {s: load(v_ref, *s) for s in slabs}

    a_b, a_k = {}, {}
    for s in slabs:
        lhs = jnp.concatenate([am[s], rp[s]], axis=0)
        a_b[s] = jnp.where(causal2, _dot_nt(lhs, blk(load(bm_ref, *s))), 0.0)
        a_k[s] = jnp.where(causal2, _dot_nt(lhs, blk(load(km_ref, *s))), 0.0)
    akv = {s: _dot(a_k[s].astype(BF16), blk(v[s])) for s in slabs}

    a_ab = {s: a_b[s][:CHUNK] for s in slabs}
    tinv = {s: jnp.where(eye_packed, 1.0, 0.0) + jnp.where(level_masks[0], a_ab[s], 0.0)
            for s in slabs}
    for lm in level_masks[1:]:
        tbf = {s: tinv[s].astype(BF16) for s in slabs}
        inner = {s: _dot(jnp.where(lm, a_ab[s], 0.0).astype(BF16), blk(tbf[s])) for s in slabs}
        tinv = {s: tinv[s] + _dot(tbf[s], blk(inner[s].astype(BF16))) for s in slabs}
    z = {s: _dot(tinv[s].astype(BF16),
                 blk(jnp.concatenate([am[s], akv[s][:CHUNK].astype(BF16)], axis=1)))
         for s in slabs}
    g2 = {s: _dot(a_b[s][CHUNK:].astype(BF16), blk(z[s].astype(BF16))) for s in slabs}

    gy, m, hadd = {}, {}, {}
    for s in slabs:
        p, c = s
        mh = _dot_tn(load(bh_ref, p, c), z[s].astype(BF16))
        kv = _dot_tn(load(kh_ref, p, c), v[s])
        pc = pc_ref[0, c:c + 1, pl.ds(p * LANES, LANES)]
        m[s] = (jnp.where(blockdiag, mh[:, :LANES], 0.0) + jnp.where(eye, pc, 0.0)).astype(BF16)
        hadd[s] = jnp.where(blockdiag, mh[:, LANES:] + kv, 0.0)
        gy[s] = ((rp[s].astype(F32) + g2[s][:, :LANES]).astype(BF16),
                 g2[s][:, LANES:] + akv[s][CHUNK:])

    hs = [h_scr[first_pair + p] for p in range(pp)]
    for c in range(nc):
        for p in range(pp):
            hb = hs[p].astype(BF16)
            gmat, y0 = gy[p, c]
            y_scr[pl.ds(c * CHUNK, CHUNK), pl.ds(p * LANES, LANES)] = _dot(gmat, hb) + y0
            hs[p] = _dot(m[p, c], hb) + hadd[p, c]
    for p in range(pp):
        h_scr[first_pair + p] = hs[p]

    avg = jnp.where(blockdiag, 1.0 / RWKV_HEAD_DIM, 0.0).astype(BF16)
    for p in range(pp):
        cols = pl.ds(p * LANES, LANES)
        y = y_scr[:, cols]
        mean = _mm_rhs_exact(y, avg)
        yc = y - mean
        var = _mm_rhs_exact(yc * yc, avg)
        yn = yc * lax.rsqrt(var + LN_X_EPS)
        out = (yn * lnw_ref[:, cols] + lnb_ref[:, cols] + bv_ref[0, :, cols]) * g_ref[0, :, cols]
        o_ref[0, :, cols] = out.astype(BF16)


def _rwkv_scan(rp, am, bm, km, bh, kh, v, pc, bv, g, lnw, lnb):
    bsz, seq, d = rp.shape
    tb = 512
    npairs = d // LANES
    pp = 4
    width = pp * LANES
    tok = pl.BlockSpec((1, tb, width), lambda b, t, p: (b, t, p))
    vec = pl.BlockSpec((1, width), lambda b, t, p: (0, p))
    return pl.pallas_call(
        _rwkv_scan_kernel,
        grid=(bsz, seq // tb, npairs // pp),
        in_specs=[tok] * 7 + [pl.BlockSpec((1, tb // CHUNK, width), lambda b, t, p: (b, t, p)),
                              tok, tok, vec, vec],
        out_specs=tok,
        out_shape=jax.ShapeDtypeStruct((bsz, seq, d), BF16),
        scratch_shapes=[pltpu.VMEM((npairs, LANES, LANES), F32),
                        pltpu.VMEM((tb, width), F32)],
        compiler_params=_cparams("arbitrary", "arbitrary", "arbitrary"),
        name="rwkv_scan",
    )(rp, am, bm, km, bh, kh, v, pc, bv, g, lnw, lnb)


def _rot_half(a, axis=-1):
    half = QK_ROPE_DIM // 2
    lo = lax.slice_in_dim(a, QK_NOPE_DIM, QK_NOPE_DIM + half, axis=axis)
    hi = lax.slice_in_dim(a, QK_NOPE_DIM + half, QK_NOPE_DIM + QK_ROPE_DIM, axis=axis)
    pads = [(0, 0)] * a.ndim
    pads[axis] = (QK_NOPE_DIM, a.shape[axis] - QK_NOPE_DIM - QK_ROPE_DIM)
    return jnp.pad(jnp.concatenate([-hi, lo], axis=axis), pads)


def _mla_heads(z, pos_ref, freq_ref, gq_ref, wuq_ref, wuqr_ref, gkv_ref, wukv_ref, wvt_ref,
               qn_ref, qnr_ref, kn_ref, knr_ref, kshift_ref, q_ref, k_ref, vt_ref, after_head):
    zq = z[:, :Q_LORA]
    zkv = z[:, Q_LORA:Q_LORA + KV_LORA]
    zpe = z[:, Q_LORA + KV_LORA:]
    ang = pos_ref[0] * freq_ref[...]
    cos = jnp.cos(ang)
    sin = jnp.sin(ang)
    lane = lax.broadcasted_iota(jnp.int32, (1, LANES), 1)
    nope = lane < QK_NOPE_DIM
    half = QK_ROPE_DIM // 2

    qa = (zq * lax.rsqrt(jnp.mean(zq * zq, axis=-1, keepdims=True) + EPS) * gq_ref[...]).astype(BF16)
    q_all = _dot(qa, wuq_ref[...])
    qr_all = _dot(qa, wuqr_ref[...])
    kva = (zkv * lax.rsqrt(jnp.mean(zkv * zkv, axis=-1, keepdims=True) + EPS) * gkv_ref[...]).astype(BF16)
    kv_all = _dot(kva, wukv_ref[...])
    pe = pltpu.roll(zpe, QK_NOPE_DIM, axis=1)
    pe_rot = jnp.where(lane < QK_NOPE_DIM + half,
                       -pltpu.roll(pe, LANES - half, axis=1), pltpu.roll(pe, half, axis=1))
    scale = QK_HEAD_DIM ** -0.5 * LOG2_E
    qc = qn_ref[...] * scale * cos
    qs = qnr_ref[...] * scale * sin
    kc = kn_ref[...] * cos
    ks = knr_ref[...] * sin
    pe_term = pe_rot * ks
    ones_row = lax.broadcasted_iota(jnp.int32, (VT_ROWS, 1), 0) == V_HEAD_DIM
    for h in range(MLA_HEADS):
        cols = slice(h * HEAD_PAD, (h + 1) * HEAD_PAD)
        q = q_all[:, cols]
        r = lax.rsqrt(jnp.sum(q * q, axis=-1, keepdims=True) * (1.0 / QK_HEAD_DIM) + EPS)
        q = r * (q * qc + qr_all[:, cols] * qs)
        q_ref[0, h] = jnp.where(lane == QK_HEAD_DIM, 1.0, q).astype(BF16)
        k = jnp.where(nope, kv_all[:, cols], pe)
        r = lax.rsqrt(jnp.sum(k * k, axis=-1, keepdims=True) * (1.0 / QK_HEAD_DIM) + EPS)
        k_ref[0, h] = (r * (k * kc + pe_term) + kshift_ref[...]).astype(BF16)
        vt = _dot_nt(wvt_ref[h], kva)
        vt_ref[0, h] = jnp.where(ones_row, 1.0, vt).astype(BF16)
        after_head(h)


def _front_kernel(x_ref, sh_ref, sc_ref, g_ref, wr_ref, wm_ref, wg_ref,
                  pos_ref, freq_ref, gq_ref, wuq_ref, wuqr_ref, gkv_ref, wukv_ref, wvt_ref,
                  qn_ref, qnr_ref, kn_ref, knr_ref, kshift_ref,
                  zr_ref, sg_ref, q_ref, k_ref, vt_ref):
    x = x_ref[0]
    ms = jnp.mean(x * x, axis=-1, keepdims=True)
    h = x * lax.rsqrt(ms + EPS) * g_ref[...]
    h = h * (1.0 + sc_ref[0]) + sh_ref[0]
    hb = h.astype(BF16)

    width = 2 * LANES
    slabs = ([(zr_ref, wr_ref, c, False) for c in range(0, zr_ref.shape[2], width)]
             + [(sg_ref, wg_ref, c, True) for c in range(0, sg_ref.shape[2], width)])
    per_head = -(-len(slabs) // (MLA_HEADS + 1))

    def project(stage):
        for out_ref, w_ref, c, gate in slabs[stage * per_head:(stage + 1) * per_head]:
            y = _dot(hb, w_ref[:, c:c + width])
            out_ref[0, :, c:c + width] = _sigmoid(y).astype(BF16) if gate else y

    zm = _dot(hb, wm_ref[...])
    project(0)
    _mla_heads(zm, pos_ref, freq_ref, gq_ref, wuq_ref, wuqr_ref, gkv_ref,
               wukv_ref, wvt_ref, qn_ref, qnr_ref, kn_ref, knr_ref, kshift_ref, q_ref, k_ref, vt_ref,
               after_head=lambda h: project(h + 1))


def _front(x, sh, sc, g, wr, wm, wg, pos, freq, gq, wuq, gkv, wukv, qn, kn, kshift):
    bsz, seq, d = x.shape
    tm = 512
    tok = lambda b, j: (b, j, 0)
    bmap = lambda b, j: (b, 0, 0)
    head = lambda b, j: (b, 0, j, 0)
    once = lambda a: pl.BlockSpec(a.shape, lambda b, j: (0,) * a.ndim, pipeline_mode=pl.Buffered(1))
    wuqr = _rot_half(wuq.reshape(Q_LORA, MLA_HEADS, HEAD_PAD)).reshape(Q_LORA, -1)
    unsign = _rot_half(jnp.ones((1, LANES), F32))
    qnr = _rot_half(qn) * unsign
    knr = _rot_half(kn) * unsign
    wv = wukv.reshape(KV_LORA, MLA_HEADS, HEAD_PAD)[:, :, QK_NOPE_DIM:]
    wvt = jnp.pad(wv.transpose(1, 2, 0), ((0, 0), (0, VT_ROWS - V_HEAD_DIM), (0, 0)))
    vec = _const_spec((1, LANES))
    return pl.pallas_call(
        _front_kernel,
        grid=(bsz, seq // tm),
        in_specs=[pl.BlockSpec((1, tm, d), tok), pl.BlockSpec((1, 1, d), bmap),
                  pl.BlockSpec((1, 1, d), bmap), _const_spec((1, d)), once(wr), once(wm), once(wg),
                  pl.BlockSpec((1, tm, 1), tok), vec, _const_spec((1, Q_LORA)), once(wuq),
                  once(wuqr), _const_spec((1, KV_LORA)), once(wukv), once(wvt),
                  vec, vec, vec, vec, vec],
        out_specs=[pl.BlockSpec((1, tm, wr.shape[1]), tok), pl.BlockSpec((1, tm, wg.shape[1]), tok),
                   pl.BlockSpec((1, MLA_HEADS, tm, HEAD_PAD), head),
                   pl.BlockSpec((1, MLA_HEADS, tm, HEAD_PAD), head),
                   pl.BlockSpec((1, MLA_HEADS, VT_ROWS, tm), lambda b, j: (b, 0, 0, j))],
        out_shape=[jax.ShapeDtypeStruct((bsz, seq, wr.shape[1]), F32),
                   jax.ShapeDtypeStruct((bsz, seq, wg.shape[1]), BF16),
                   jax.ShapeDtypeStruct((bsz, MLA_HEADS, seq, HEAD_PAD), BF16),
                   jax.ShapeDtypeStruct((bsz, MLA_HEADS, seq, HEAD_PAD), BF16),
                   jax.ShapeDtypeStruct((bsz, MLA_HEADS, VT_ROWS, seq), BF16)],
        compiler_params=_cparams("arbitrary", "arbitrary"),
        name="front",
    )(x, sh, sc, g, wr, wm, wg, pos, freq, gq, wuq, wuqr, gkv, wukv, wvt, qn, qnr, kn, knr, kshift)


def _attn_kernel(qi_ref, kj_ref, fixed_ref, q_ref, k_ref, vt_ref, o_ref, m_scr, acc_scr):
    p = pl.program_id(1)
    qi = qi_ref[p]
    kj = kj_ref[p]
    fixed_shift = fixed_ref[0] == 1
    bq = q_ref.shape[2]
    bk = k_ref.shape[2]
    heads = range(MLA_HEADS)

    @pl.when(kj == 0)
    def _():
        m_scr[...] = jnp.full(m_scr.shape, NEG_BIG, F32)
        acc_scr[...] = jnp.zeros(acc_scr.shape, F32)

    def scores(diagonal):
        st = [_dot_nt(k_ref[0, h], q_ref[0, h]) for h in heads]
        if diagonal:
            visible = (lax.broadcasted_iota(jnp.int32, (bk, bq), 0)
                       <= lax.broadcasted_iota(jnp.int32, (bk, bq), 1))
            st = [jnp.where(visible, x, NEG_BIG) for x in st]
        return st

    def block_fixed(diagonal):
        st = scores(diagonal)
        pr = [jnp.exp2(st[h]).astype(BF16) for h in heads]
        pv = [_dot(vt_ref[0, h], pr[h]) for h in heads]
        for h in heads:
            acc_scr[h] += pv[h]

    def block(diagonal):
        st = scores(diagonal)
        m_prev = [m_scr[h] for h in heads]
        m_new = [jnp.maximum(m_prev[h], jnp.max(st[h], axis=0, keepdims=True)) for h in heads]
        pr = [jnp.exp2(st[h] - m_new[h]).astype(BF16) for h in heads]
        alpha = [jnp.exp2(m_prev[h] - m_new[h]) for h in heads]
        pv = [_dot(vt_ref[0, h], pr[h]) for h in heads]
        for h in heads:
            acc_scr[h] = alpha[h] * acc_scr[h] + pv[h]
            m_scr[h] = m_new[h]

    for diagonal in (False, True):
        on_diag = (kj == qi) if diagonal else (kj != qi)
        pl.when(on_diag & fixed_shift)(functools.partial(block_fixed, diagonal))
        pl.when(on_diag & jnp.logical_not(fixed_shift))(functools.partial(block, diagonal))

    @pl.when(kj == qi)
    def _():
        for h in heads:
            acc = acc_scr[h]
            o = acc[:V_HEAD_DIM] * (1.0 / acc[V_HEAD_DIM:V_HEAD_DIM + 1])
            o_ref[0, h * V_HEAD_DIM:(h + 1) * V_HEAD_DIM, :] = o.astype(BF16)


def _attn(q, k, v, fixed):
    bsz, nh, seq, _ = q.shape
    bq = bk = 512
    nq = seq // bq
    qi = np.concatenate([np.full(i + 1, i) for i in range(nq)]).astype(np.int32)
    kj = np.concatenate([np.arange(i + 1) for i in range(nq)]).astype(np.int32)
    grid_spec = pltpu.PrefetchScalarGridSpec(
        num_scalar_prefetch=3,
        grid=(bsz, len(qi)),
        in_specs=[pl.BlockSpec((1, nh, bq, HEAD_PAD), lambda b, p, qi, kj, f: (b, 0, qi[p], 0)),
                  pl.BlockSpec((1, nh, bk, HEAD_PAD), lambda b, p, qi, kj, f: (b, 0, kj[p], 0)),
                  pl.BlockSpec((1, nh, VT_ROWS, bk), lambda b, p, qi, kj, f: (b, 0, 0, kj[p]))],
        out_specs=pl.BlockSpec((1, nh * V_HEAD_DIM, bq), lambda b, p, qi, kj, f: (b, 0, qi[p])),
        scratch_shapes=[pltpu.VMEM((nh, 1, bq), F32), pltpu.VMEM((nh, VT_ROWS, bq), F32)],
    )
    return pl.pallas_call(
        _attn_kernel,
        grid_spec=grid_spec,
        out_shape=jax.ShapeDtypeStruct((bsz, nh * V_HEAD_DIM, seq), BF16),
        compiler_params=_cparams("arbitrary", "arbitrary"),
        name="mla_attn",
    )(jnp.asarray(qi), jnp.asarray(kj), fixed, q, k, v)


def _first_index_of_max(vals, lane):
    mx = jnp.max(vals, axis=-1, keepdims=True)
    idx = jnp.min(jnp.where(vals == mx, lane, LANES), axis=-1, keepdims=True)
    return mx, idx


def _post_kernel(yr_ref, ymt_ref, sg_ref, x_ref, gt_ref, sh_ref, sc_ref, g_ref,
                 wb0_ref, wb1_ref, wo_ref, wr_ref, br_ref, x1_ref, h2_ref, cmb_ref, cnt_ref):
    sg = sg_ref[...].astype(F32)
    merged = (sg[:, :D_MODEL] * _dot(yr_ref[...], wb0_ref[...])
              + sg[:, D_MODEL:] * _dot_tn(ymt_ref[0], wb1_ref[...]))
    x1 = x_ref[...] + gt_ref[0] * _mm(merged, wo_ref[...])
    x1_ref[...] = x1
    ms = jnp.mean(x1 * x1, axis=-1, keepdims=True)
    h2 = x1 * lax.rsqrt(ms + EPS) * g_ref[...]
    h2 = h2 * (1.0 + sc_ref[0]) + sh_ref[0]
    h2_ref[...] = h2.astype(BF16)

    logits = _mm3(h2, wr_ref[...]) + br_ref[...]
    lane = lax.broadcasted_iota(jnp.int32, logits.shape, 1).astype(F32)
    gl = jnp.where((lane >= N_EXPERTS) & (lane < N_EXPERTS + N_GROUPS), logits, NEG_BIG)
    gmax, gidx = _first_index_of_max(gl, lane)
    p_g = 1.0 / jnp.sum(jnp.exp(gl - gmax), axis=-1, keepdims=True)
    first = (gidx - N_EXPERTS) * EXPERTS_PER_GROUP
    el = jnp.where((lane >= first) & (lane < first + EXPERTS_PER_GROUP), logits, NEG_BIG)
    m1, i1 = _first_index_of_max(el, lane)
    z = jnp.sum(jnp.exp(el - m1), axis=-1, keepdims=True)
    el2 = jnp.where(lane == i1, NEG_BIG, el)
    m2, i2 = _first_index_of_max(el2, lane)
    p1 = 1.0 / z
    p2 = jnp.exp(m2 - m1) / z
    tot = p1 + p2
    group_onehot = jnp.where(lane == gidx, 1.0, 0.0)
    cmb_ref[...] = (jnp.where(lane == i1, p1 / tot * p_g, 0.0)
                    + jnp.where(lane == i2, p2 / tot * p_g, 0.0) + group_onehot)
    cnt_ref[0] = jnp.broadcast_to(jnp.sum(group_onehot, axis=0, keepdims=True), cnt_ref.shape[1:])


def _post(yr, ym, sg, x2, gt, sh, sc, g, wb0, wb1, wo, wr, br, seq):
    n, d = x2.shape
    tm = 512
    per_b = seq // tm
    bmap = lambda i: (i // per_b, 0, 0)
    rowmap = lambda i: (i, 0)
    return pl.pallas_call(
        _post_kernel,
        grid=(n // tm,),
        in_specs=[pl.BlockSpec((tm, D_RWKV), rowmap),
                  pl.BlockSpec((1, D_MLA, tm), lambda i: (i // per_b, 0, i % per_b)),
                  pl.BlockSpec((tm, 2 * d), rowmap), pl.BlockSpec((tm, d), rowmap),
                  pl.BlockSpec((1, 1, d), bmap), pl.BlockSpec((1, 1, d), bmap),
                  pl.BlockSpec((1, 1, d), bmap), _const_spec((1, d)),
                  _const_spec(wb0.shape), _const_spec(wb1.shape), _const_spec(wo.shape),
                  _const_spec(wr.shape), _const_spec(br.shape)],
        out_specs=[pl.BlockSpec((tm, d), rowmap), pl.BlockSpec((tm, d), rowmap),
                   pl.BlockSpec((tm, LANES), rowmap),
                   pl.BlockSpec((1, 8, LANES), lambda i: (i, 0, 0))],
        out_shape=[jax.ShapeDtypeStruct((n, d), F32), jax.ShapeDtypeStruct((n, d), BF16),
                   jax.ShapeDtypeStruct((n, LANES), F32),
                   jax.ShapeDtypeStruct((n // tm, 8, LANES), F32)],
        compiler_params=_cparams("arbitrary"),
        name="post",
    )(yr, ym, sg, x2, gt, sh, sc, g, wb0, wb1, wo, wr, br)


MOE_TILE = 1024
MOE_CHUNK = 256


def _moe_kernel(cnt_ref, h_ref, cmb_ref, x1_ref, gt_ref, wg_ref, wu_ref, wd_ref, o_ref,
                key_scr):
    i = pl.program_id(0)
    g = pl.program_id(1)
    tm = h_ref.shape[0]
    lane = lax.broadcasted_iota(jnp.int32, (1, LANES), 1)

    @pl.when(g == 0)
    def _():
        o_ref[...] = x1_ref[...]
        onehot = jnp.where((lane >= N_EXPERTS) & (lane < N_EXPERTS + N_GROUPS), cmb_ref[...], 0.0)
        before = (lax.broadcasted_iota(jnp.int32, (tm, tm), 1)
                  < lax.broadcasted_iota(jnp.int32, (tm, tm), 0))
        rank = _dot(jnp.where(before, 1.0, 0.0).astype(BF16), onehot.astype(BF16))
        key = jnp.where(onehot > 0.5, rank, -1.0)
        key_scr[...] = key.T

    key_row = key_scr[pl.ds(N_EXPERTS + g, 1), :]
    h = h_ref[...]
    cmb = cmb_ref[...]
    cmb_hi, cmb_lo = _split(cmb)
    gt = gt_ref[0]
    first_lane = g * EXPERTS_PER_GROUP

    def run_chunk(first_slot, rows):
        slot = (first_slot + lax.broadcasted_iota(jnp.int32, (rows, 1), 0)).astype(F32)
        sel = jnp.where(key_row == slot, 1.0, 0.0).astype(BF16)
        xg = _dot(sel, h).astype(BF16)
        cw = _dot(sel, cmb_hi) + _dot(sel, cmb_lo)
        y = jnp.zeros((rows, o_ref.shape[1]), F32)
        for e in range(EXPERTS_PER_GROUP):
            gate = _dot(xg, wg_ref[e])
            hid = gate * _sigmoid(gate) * _dot(xg, wu_ref[e])
            cwe = jnp.sum(jnp.where(lane == first_lane + e, cw, 0.0), axis=-1, keepdims=True)
            y = y + _mm(hid * cwe, wd_ref[e])
        o_ref[...] += _dot_tn(sel, (y * gt).astype(BF16))

    cnt = cnt_ref[i * N_GROUPS + g]
    n_full = cnt // MOE_CHUNK
    rem = cnt - n_full * MOE_CHUNK
    n_full = n_full + (rem > MOE_CHUNK // 2).astype(jnp.int32)

    def full(c, carry):
        run_chunk(c * MOE_CHUNK, MOE_CHUNK)
        return carry

    lax.fori_loop(0, n_full, full, 0)

    @pl.when((rem > 0) & (rem <= MOE_CHUNK // 2))
    def _():
        run_chunk(n_full * MOE_CHUNK, MOE_CHUNK // 2)


def _moe(cnt, h2, cmb, x1, gt, wg, wu, wd, seq):
    n, d = x1.shape
    tm = MOE_TILE
    per_b = seq // tm
    rowmap = lambda i, g, cnt: (i, 0)
    wmap = lambda i, g, cnt: (g, 0, 0)
    grid_spec = pltpu.PrefetchScalarGridSpec(
        num_scalar_prefetch=1,
        grid=(n // tm, N_GROUPS),
        in_specs=[pl.BlockSpec((tm, d), rowmap), pl.BlockSpec((tm, LANES), rowmap),
                  pl.BlockSpec((tm, d), rowmap),
                  pl.BlockSpec((1, 1, d), lambda i, g, cnt: (i // per_b, 0, 0)),
                  pl.BlockSpec((EXPERTS_PER_GROUP, d, D_EXPERT), wmap),
                  pl.BlockSpec((EXPERTS_PER_GROUP, d, D_EXPERT), wmap),
                  pl.BlockSpec((EXPERTS_PER_GROUP, D_EXPERT, d), wmap)],
        out_specs=pl.BlockSpec((tm, d), rowmap),
        scratch_shapes=[pltpu.VMEM((LANES, tm), F32)],
    )
    return pl.pallas_call(
        _moe_kernel,
        grid_spec=grid_spec,
        out_shape=jax.ShapeDtypeStruct((n, d), F32),
        compiler_params=_cparams("arbitrary", "arbitrary"),
        name="moe",
    )(cnt, h2, cmb, x1, gt, wg, wu, wd)


def _pad_cols(w, n):
    return jnp.pad(w, ((0, 0), (0, n - w.shape[1])))


def kernel(x, c, positions, w_ada, b_ada, g_norm_mix, w_in, mu_shift, w0, w_decay_up, a0, w_a_up, w_g_up, k_k, k_a, r_k, ln_x_w, ln_x_b, g_q_a, w_uq, g_kv_a, w_ukv, q_norm, k_norm, w_branch, w_out, g_norm_ffn, w_router_group, b_router_group, w_router_expert, b_router_expert, w_e_gate, w_e_up, w_e_down):
    bsz, seq, d = x.shape
    n = bsz * seq
    row = lambda a: a.reshape(1, -1)

    freqs = ROPE_THETA ** (-(jnp.arange(0, QK_ROPE_DIM, 2, dtype=F32) / QK_ROPE_DIM))
    freq = jnp.zeros((1, LANES), F32).at[0, QK_NOPE_DIM:QK_NOPE_DIM + QK_ROPE_DIM].set(
        jnp.concatenate([freqs, freqs]))
    pos = positions.astype(F32).reshape(bsz, seq, 1)

    x2 = x.reshape(n, d)
    for l in range(w_ada.shape[0]):
        mod = _ada(c, w_ada[l], b_ada[l])
        sh_m, sc_m, gt_m, sh_f, sc_f, gt_f = [m.reshape(bsz, 1, d) for m in jnp.split(mod, 6, axis=-1)]

        w_in_l = w_in[l].astype(BF16)
        wr = w_in_l[:, :RWKV_COLS]
        wm = _pad_cols(w_in_l[:, RWKV_COLS:RWKV_COLS + MLA_COLS], MLA_COLS_PAD)
        wg = w_in_l[:, RWKV_COLS + MLA_COLS:]
        wuq = w_uq[l].reshape(Q_LORA, MLA_HEADS, QK_HEAD_DIM)
        wuq = jnp.pad(wuq, ((0, 0), (0, 0), (0, HEAD_PAD - QK_HEAD_DIM))).reshape(Q_LORA, -1)
        qn = _pad_cols(row(q_norm[l]), LANES)
        kn = _pad_cols(row(k_norm[l]), LANES)
        bound = (QK_HEAD_DIM * jnp.max(jnp.abs(q_norm[l])) * jnp.max(jnp.abs(k_norm[l]))
                 * QK_HEAD_DIM ** -0.5 * LOG2_E)
        fixed = 2.0 * bound <= MAX_FIXED_SHIFT_SPAN
        kshift = jnp.zeros((1, LANES), F32).at[0, QK_HEAD_DIM].set(jnp.where(fixed, -bound, 0.0))
        zr, sg, q, k, v = _front(x2.reshape(bsz, seq, d), sh_m, sc_m, row(g_norm_mix[l]), wr, wm, wg,
                                 pos, freq, row(g_q_a[l]), wuq.astype(BF16), row(g_kv_a[l]),
                                 w_ukv[l].astype(BF16), qn, kn, kshift)
        sg = sg.reshape(n, -1)

        zeros_lora = jnp.zeros((DECAY_LORA, D_RWKV), F32)
        wdec = jnp.concatenate([w_decay_up[l], zeros_lora], axis=0)
        wa = jnp.concatenate([zeros_lora, w_a_up[l]], axis=0)
        prep = _rwkv_prep(zr.reshape(bsz, seq, RWKV_COLS), row(mu_shift[l]), row(w0[l]), wdec,
                          row(a0[l]), wa, w_g_up[l], row(k_k[l]), row(k_a[l]), row(r_k[l]))
        y_rwkv = _rwkv_scan(*prep, row(ln_x_w[l]), row(ln_x_b[l]))

        y_mla = _attn(q, k, v, fixed.astype(jnp.int32).reshape(1))

        w_router = _pad_cols(jnp.concatenate([w_router_expert[l], w_router_group[l]], axis=1), LANES)
        b_router = _pad_cols(row(jnp.concatenate([b_router_expert[l], b_router_group[l]])), LANES)
        x1, h2, cmb, cnt = _post(y_rwkv.reshape(n, D_RWKV), y_mla, sg, x2,
                                 gt_m, sh_f, sc_f, row(g_norm_ffn[l]),
                                 w_branch[l, 0].astype(BF16), w_branch[l, 1].astype(BF16),
                                 w_out[l].astype(BF16), w_router, b_router, seq)

        cnt = cnt[:, 0, N_EXPERTS:N_EXPERTS + N_GROUPS].reshape(n // MOE_TILE, -1, N_GROUPS)
        cnt = jnp.sum(cnt, axis=1).astype(jnp.int32).reshape(-1)
        x2 = _moe(cnt, h2, cmb, x1, gt_f, w_e_gate[l].astype(BF16), w_e_up[l].astype(BF16),
                  w_e_down[l].astype(BF16), seq)
    return x2.reshape(bsz, seq, d)
```

```python
import functools

import numpy as np
import jax
import jax.numpy as jnp
from jax import lax
from jax.experimental import pallas as pl
from jax.experimental.pallas import tpu as pltpu

F32 = jnp.float32
BF16 = jnp.bfloat16

D_MODEL = 1024
EPS = 1e-6
RWKV_HEADS = 8
RWKV_HEAD_DIM = 64
D_RWKV = 512
DECAY_LORA = 64
AAA_LORA = 64
GATE_LORA = 128
LN_X_EPS = 64e-5
MLA_HEADS = 8
QK_NOPE_DIM = 64
QK_ROPE_DIM = 32
QK_HEAD_DIM = 96
V_HEAD_DIM = 64
D_MLA = 512
Q_LORA = 256
KV_LORA = 128
ROPE_THETA = 10000.0
RWKV_COLS = 1792
MLA_COLS = 416
MLA_COLS_PAD = 512
N_GROUPS = 4
EXPERTS_PER_GROUP = 8
N_EXPERTS = 32
D_EXPERT = 256

LANES = 128
CHUNK = 64
HEAD_PAD = 128
VT_ROWS = 128
NEG_BIG = -1e30
LOG2_E = 1.4426950408889634
MAX_FIXED_SHIFT_SPAN = 80.0
VMEM_LIMIT = 56 * 1024 * 1024


def _cparams(*sem):
    return pltpu.CompilerParams(dimension_semantics=sem, vmem_limit_bytes=VMEM_LIMIT)


def _dot(a, b):
    return jnp.dot(a, b, preferred_element_type=F32)


def _dot_nt(a, b):
    return lax.dot_general(a, b, (((1,), (1,)), ((), ())), preferred_element_type=F32)


def _dot_tn(a, b):
    return lax.dot_general(a, b, (((0,), (0,)), ((), ())), preferred_element_type=F32)


def _mm(a, b):
    return _dot(a.astype(BF16), b.astype(BF16))


def _split(a):
    hi = a.astype(BF16)
    lo = (a - hi.astype(F32)).astype(BF16)
    return hi, lo


def _mm_rhs_exact(a, b):
    hi, lo = _split(a)
    return _dot(hi, b) + _dot(lo, b)


def _mm_lhs_exact(a, b):
    hi, lo = _split(b)
    return _dot(a, hi) + _dot(a, lo)


def _mm3(a, b):
    ah, al = _split(a)
    bh, bl = _split(b)
    return _dot(ah, bh) + (_dot(ah, bl) + _dot(al, bh))


def _sigmoid(x):
    return 1.0 / (1.0 + jnp.exp(-x))


def _const_spec(shape):
    nd = len(shape)
    return pl.BlockSpec(shape, lambda *_: (0,) * nd)


def _ada_kernel(c_ref, w_ref, b_ref, o_ref):
    c = c_ref[...]
    s = c * _sigmoid(c)
    o_ref[...] = _mm3(s, w_ref[...]) + b_ref[...]


def _ada(c, w, b):
    bsz, d = c.shape
    n = w.shape[1]
    tn = 1024
    return pl.pallas_call(
        _ada_kernel,
        grid=(n // tn,),
        in_specs=[pl.BlockSpec((bsz, d), lambda j: (0, 0)),
                  pl.BlockSpec((d, tn), lambda j: (0, j)),
                  pl.BlockSpec((1, tn), lambda j: (0, j))],
        out_specs=pl.BlockSpec((bsz, tn), lambda j: (0, j)),
        out_shape=jax.ShapeDtypeStruct((bsz, n), F32),
        compiler_params=_cparams("arbitrary"),
        name="ada",
    )(c, w, b.reshape(1, n))


def _rwkv_prep_kernel(z_ref, prev_ref, mu_ref, w0_ref, wdec_ref, a0_ref, wa_ref, wgu_ref,
                      kk_ref, ka_ref, rk_ref, ones_ref, tri_ref, blk_ref,
                      rp_ref, am_ref, bm_ref, km_ref, bh_ref, kh_ref, v_ref,
                      pc_ref, bv_ref, g_ref):
    tt = z_ref.shape[1]
    z = z_ref[0]
    prev = prev_ref[0][7:8, :]
    prev = jnp.where(pl.program_id(1) == 0, 0.0, prev)
    row = lax.broadcasted_iota(jnp.int32, (tt, 1), 0)
    zs = jnp.where(row == 0, prev, pltpu.roll(z, 1, axis=0))
    z = z + (zs - z) * mu_ref[...]
    zr = z[:, 0:D_RWKV]
    zk = z[:, D_RWKV:2 * D_RWKV]
    zv = z[:, 2 * D_RWKV:3 * D_RWKV]
    zwa = z[:, 3 * D_RWKV:3 * D_RWKV + DECAY_LORA + AAA_LORA]
    zg = z[:, 3 * D_RWKV + DECAY_LORA + AAA_LORA:]

    u = -(w0_ref[...] + _mm3(jnp.tanh(zwa), wdec_ref[...]))
    softplus = jnp.maximum(u, 0.0) + jnp.log(1.0 + jnp.exp(-jnp.abs(u)))
    logw = -jnp.exp(-softplus - 0.5)
    a = _sigmoid(a0_ref[...] + _mm(zwa, wa_ref[...]))
    g = _mm(_sigmoid(zg), wgu_ref[...])

    ones_blk = ones_ref[...]
    xk = zk * kk_ref[...]
    ss = _mm_rhs_exact(xk * xk, ones_blk)
    kk = xk * lax.rsqrt(jnp.maximum(ss, 1e-24))
    k = zk * (1.0 + (a - 1.0) * ka_ref[...])
    b = kk * a
    bonus = _mm_rhs_exact(zr * k * rk_ref[...], ones_blk)

    cum = _mm_lhs_exact(tri_ref[...], logw)
    tot = _mm_lhs_exact(blk_ref[...], logw)
    e_neg = jnp.exp(-cum)
    e_tot = jnp.exp(tot)
    bm = b * e_neg
    km = k * e_neg
    rp_ref[0] = (zr * jnp.exp(cum)).astype(BF16)
    am_ref[0] = (-kk * jnp.exp(cum - logw)).astype(BF16)
    bm_ref[0] = bm.astype(BF16)
    km_ref[0] = km.astype(BF16)
    bh_ref[0] = (bm * e_tot).astype(BF16)
    kh_ref[0] = (km * e_tot).astype(BF16)
    v_ref[0] = zv.astype(BF16)
    bv_ref[0] = bonus * zv
    g_ref[0] = g
    nc = tt // CHUNK
    pick = (lax.broadcasted_iota(jnp.int32, (nc, tt), 1)
            == CHUNK * lax.broadcasted_iota(jnp.int32, (nc, tt), 0))
    pc_ref[0] = _mm_lhs_exact(jnp.where(pick, 1.0, 0.0).astype(BF16), e_tot)


def _rwkv_prep(zr3, mu, w0, wdec, a0, wa, wgu, k_k, k_a, r_k):
    bsz, seq, cols = zr3.shape
    tt = 512
    d = D_RWKV
    ones_blk = jnp.asarray(np.kron(np.eye(RWKV_HEADS), np.ones((RWKV_HEAD_DIM, RWKV_HEAD_DIM))), BF16)
    cidx = np.arange(tt) // CHUNK
    same = cidx[:, None] == cidx[None, :]
    tri = jnp.asarray(same & (np.arange(tt)[None, :] <= np.arange(tt)[:, None]), BF16)
    blk = jnp.asarray(same, BF16)
    tok = lambda b, j: (b, j, 0)
    big = pl.BlockSpec((1, tt, d), tok)
    out_bf = jax.ShapeDtypeStruct((bsz, seq, d), BF16)
    out_f = jax.ShapeDtypeStruct((bsz, seq, d), F32)
    return pl.pallas_call(
        _rwkv_prep_kernel,
        grid=(bsz, seq // tt),
        in_specs=[pl.BlockSpec((1, tt, cols), tok),
                  pl.BlockSpec((1, 8, cols), lambda b, j: (b, jnp.maximum(j * (tt // 8) - 1, 0), 0)),
                  _const_spec((1, cols)), _const_spec((1, d)), _const_spec(wdec.shape),
                  _const_spec((1, d)), _const_spec(wa.shape), _const_spec(wgu.shape),
                  _const_spec((1, d)), _const_spec((1, d)), _const_spec((1, d)),
                  _const_spec((d, d)), _const_spec((tt, tt)), _const_spec((tt, tt))],
        out_specs=[big] * 7 + [pl.BlockSpec((1, tt // CHUNK, d), tok), big, big],
        out_shape=[out_bf] * 7 + [jax.ShapeDtypeStruct((bsz, seq // CHUNK, d), F32), out_f, out_f],
        compiler_params=_cparams("arbitrary", "arbitrary"),
        name="rwkv_prep",
    )(zr3, zr3, mu, w0, wdec, a0, wa, wgu, k_k, k_a, r_k, ones_blk, tri, blk)


def _rwkv_scan_kernel(rp_ref, am_ref, bm_ref, km_ref, bh_ref, kh_ref, v_ref, pc_ref,
                      bv_ref, g_ref, lnw_ref, lnb_ref, o_ref, h_scr, y_scr):
    tb = rp_ref.shape[1]
    nc = tb // CHUNK
    pp = rp_ref.shape[2] // LANES
    first_pair = pl.program_id(2) * pp

    @pl.when(pl.program_id(1) == 0)
    def _():
        for p in range(pp):
            h_scr[first_pair + p] = jnp.zeros((LANES, LANES), F32)

    lane = lax.broadcasted_iota(jnp.int32, (1, LANES), 1)
    lane2 = lax.broadcasted_iota(jnp.int32, (1, 2 * LANES), 1)
    in_head = [lane < RWKV_HEAD_DIM, lane >= RWKV_HEAD_DIM]
    in_head2 = [(lane2 % LANES) < RWKV_HEAD_DIM, (lane2 % LANES) >= RWKV_HEAD_DIM]

    def blk(x):
        masks = in_head if x.shape[1] == LANES else in_head2
        zero = jnp.zeros((), x.dtype)
        return jnp.concatenate([jnp.where(masks[0], x, zero), jnp.where(masks[1], x, zero)], axis=0)

    rr = lax.broadcasted_iota(jnp.int32, (CHUNK, LANES), 0)
    cc = lax.broadcasted_iota(jnp.int32, (CHUNK, LANES), 1) % CHUNK
    eye_packed = cc == rr
    r2 = lax.broadcasted_iota(jnp.int32, (2 * CHUNK, LANES), 0)
    c2 = lax.broadcasted_iota(jnp.int32, (2 * CHUNK, LANES), 1) % CHUNK
    causal2 = ((r2 < CHUNK) & (c2 < r2)) | ((r2 >= CHUNK) & (c2 <= r2 - CHUNK))
    level_masks = [((rr >> (k + 1)) == (cc >> (k + 1))) & ((rr >> k) == (cc >> k) + 1)
                   for k in range(6)]
    r128 = lax.broadcasted_iota(jnp.int32, (LANES, LANES), 0)
    c128 = lax.broadcasted_iota(jnp.int32, (LANES, LANES), 1)
    blockdiag = (r128 < RWKV_HEAD_DIM) == (c128 < RWKV_HEAD_DIM)
    eye = r128 == c128

    slabs = [(p, c) for p in range(pp) for c in range(nc)]

    def load(ref, p, c):
        return ref[0, pl.ds(c * CHUNK, CHUNK), pl.ds(p * LANES, LANES)]

    rp = {s: load(rp_ref, *s) for s in slabs}
    am = {s: load(am_ref, *s) for s in slabs}
    v = {s: load(v_ref, *s) for s in slabs}

    a_b, a_k = {}, {}
    for s in slabs:
        lhs = jnp.concatenate([am[s], rp[s]], axis=0)
        a_b[s] = jnp.where(causal2, _dot_nt(lhs, blk(load(bm_ref, *s))), 0.0)
        a_k[s] = jnp.where(causal2, _dot_nt(lhs, blk(load(km_ref, *s))), 0.0)
    akv = {s: _dot(a_k[s].astype(BF16), blk(v[s])) for s in slabs}

    a_ab = {s: a_b[s][:CHUNK] for s in slabs}
    tinv = {s: jnp.where(eye_packed, 1.0, 0.0) + jnp.where(level_masks[0], a_ab[s], 0.0)
            for s in slabs}
    for lm in level_masks[1:]:
        tbf = {s: tinv[s].astype(BF16) for s in slabs}
        inner = {s: _dot(jnp.where(lm, a_ab[s], 0.0).astype(BF16), blk(tbf[s])) for s in slabs}
        tinv = {s: tinv[s] + _dot(tbf[s], blk(inner[s].astype(BF16))) for s in slabs}
    z = {s: _dot(tinv[s].astype(BF16),
                 blk(jnp.concatenate([am[s], akv[s][:CHUNK].astype(BF16)], axis=1)))
         for s in slabs}
    g2 = {s: _dot(a_b[s][CHUNK:].astype(BF16), blk(z[s].astype(BF16))) for s in slabs}

    gy, m, hadd = {}, {}, {}
    for s in slabs:
        p, c = s
        mh = _dot_tn(load(bh_ref, p, c), z[s].astype(BF16))
        kv = _dot_tn(load(kh_ref, p, c), v[s])
        pc = pc_ref[0, c:c + 1, pl.ds(p * LANES, LANES)]
        m[s] = (jnp.where(blockdiag, mh[:, :LANES], 0.0) + jnp.where(eye, pc, 0.0)).astype(BF16)
        hadd[s] = jnp.where(blockdiag, mh[:, LANES:] + kv, 0.0)
        gy[s] = ((rp[s].astype(F32) + g2[s][:, :LANES]).astype(BF16),
                 g2[s][:, LANES:] + akv[s][CHUNK:])

    hs = [h_scr[first_pair + p] for p in range(pp)]
    for c in range(nc):
        for p in range(pp):
            hb = hs[p].astype(BF16)
            gmat, y0 = gy[p, c]
            y_scr[pl.ds(c * CHUNK, CHUNK), pl.ds(p * LANES, LANES)] = _dot(gmat, hb) + y0
            hs[p] = _dot(m[p, c], hb) + hadd[p, c]
    for p in range(pp):
        h_scr[first_pair + p] = hs[p]

    avg = jnp.where(blockdiag, 1.0 / RWKV_HEAD_DIM, 0.0).astype(BF16)
    for p in range(pp):
        cols = pl.ds(p * LANES, LANES)
        y = y_scr[:, cols]
        mean = _mm_rhs_exact(y, avg)
        yc = y - mean
        var = _mm_rhs_exact(yc * yc, avg)
        yn = yc * lax.rsqrt(var + LN_X_EPS)
        out = (yn * lnw_ref[:, cols] + lnb_ref[:, cols] + bv_ref[0, :, cols]) * g_ref[0, :, cols]
        o_ref[0, :, cols] = out.astype(BF16)


def _rwkv_scan(rp, am, bm, km, bh, kh, v, pc, bv, g, lnw, lnb):
    bsz, seq, d = rp.shape
    tb = 512
    npairs = d // LANES
    pp = 4
    width = pp * LANES
    tok = pl.BlockSpec((1, tb, width), lambda b, t, p: (b, t, p))
    vec = pl.BlockSpec((1, width), lambda b, t, p: (0, p))
    return pl.pallas_call(
        _rwkv_scan_kernel,
        grid=(bsz, seq // tb, npairs // pp),
        in_specs=[tok] * 7 + [pl.BlockSpec((1, tb // CHUNK, width), lambda b, t, p: (b, t, p)),
                              tok, tok, vec, vec],
        out_specs=tok,
        out_shape=jax.ShapeDtypeStruct((bsz, seq, d), BF16),
        scratch_shapes=[pltpu.VMEM((npairs, LANES, LANES), F32),
                        pltpu.VMEM((tb, width), F32)],
        compiler_params=_cparams("arbitrary", "arbitrary", "arbitrary"),
        name="rwkv_scan",
    )(rp, am, bm, km, bh, kh, v, pc, bv, g, lnw, lnb)


def _rot_half(a, axis=-1):
    half = QK_ROPE_DIM // 2
    lo = lax.slice_in_dim(a, QK_NOPE_DIM, QK_NOPE_DIM + half, axis=axis)
    hi = lax.slice_in_dim(a, QK_NOPE_DIM + half, QK_NOPE_DIM + QK_ROPE_DIM, axis=axis)
    pads = [(0, 0)] * a.ndim
    pads[axis] = (QK_NOPE_DIM, a.shape[axis] - QK_NOPE_DIM - QK_ROPE_DIM)
    return jnp.pad(jnp.concatenate([-hi, lo], axis=axis), pads)


def _mla_heads(z, pos_ref, freq_ref, gq_ref, wuq_ref, wuqr_ref, gkv_ref, wukv_ref, wvt_ref,
               qn_ref, qnr_ref, kn_ref, knr_ref, kshift_ref, q_ref, k_ref, vt_ref, after_head):
    zq = z[:, :Q_LORA]
    zkv = z[:, Q_LORA:Q_LORA + KV_LORA]
    zpe = z[:, Q_LORA + KV_LORA:]
    ang = pos_ref[0] * freq_ref[...]
    cos = jnp.cos(ang)
    sin = jnp.sin(ang)
    lane = lax.broadcasted_iota(jnp.int32, (1, LANES), 1)
    nope = lane < QK_NOPE_DIM
    half = QK_ROPE_DIM // 2

    qa = (zq * lax.rsqrt(jnp.mean(zq * zq, axis=-1, keepdims=True) + EPS) * gq_ref[...]).astype(BF16)
    q_all = _dot(qa, wuq_ref[...])
    qr_all = _dot(qa, wuqr_ref[...])
    kva = (zkv * lax.rsqrt(jnp.mean(zkv * zkv, axis=-1, keepdims=True) + EPS) * gkv_ref[...]).astype(BF16)
    kv_all = _dot(kva, wukv_ref[...])
    pe = pltpu.roll(zpe, QK_NOPE_DIM, axis=1)
    pe_rot = jnp.where(lane < QK_NOPE_DIM + half,
                       -pltpu.roll(pe, LANES - half, axis=1), pltpu.roll(pe, half, axis=1))
    scale = QK_HEAD_DIM ** -0.5 * LOG2_E
    qc = qn_ref[...] * scale * cos
    qs = qnr_ref[...] * scale * sin
    kc = kn_ref[...] * cos
    ks = knr_ref[...] * sin
    pe_term = pe_rot * ks
    ones_row = lax.broadcasted_iota(jnp.int32, (VT_ROWS, 1), 0) == V_HEAD_DIM
    for h in range(MLA_HEADS):
        cols = slice(h * HEAD_PAD, (h + 1) * HEAD_PAD)
        q = q_all[:, cols]
        r = lax.rsqrt(jnp.sum(q * q, axis=-1, keepdims=True) * (1.0 / QK_HEAD_DIM) + EPS)
        q = r * (q * qc + qr_all[:, cols] * qs)
        q_ref[0, h] = jnp.where(lane == QK_HEAD_DIM, 1.0, q).astype(BF16)
        k = jnp.where(nope, kv_all[:, cols], pe)
        r = lax.rsqrt(jnp.sum(k * k, axis=-1, keepdims=True) * (1.0 / QK_HEAD_DIM) + EPS)
        k_ref[0, h] = (r * (k * kc + pe_term) + kshift_ref[...]).astype(BF16)
        vt = _dot_nt(wvt_ref[h], kva)
        vt_ref[0, h] = jnp.where(ones_row, 1.0, vt).astype(BF16)
        after_head(h)


def _front_kernel(x_ref, sh_ref, sc_ref, g_ref, wr_ref, wm_ref, wg_ref,
                  pos_ref, freq_ref, gq_ref, wuq_ref, wuqr_ref, gkv_ref, wukv_ref, wvt_ref,
                  qn_ref, qnr_ref, kn_ref, knr_ref, kshift_ref,
                  zr_ref, sg_ref, q_ref, k_ref, vt_ref):
    x = x_ref[0]
    ms = jnp.mean(x * x, axis=-1, keepdims=True)
    h = x * lax.rsqrt(ms + EPS) * g_ref[...]
    h = h * (1.0 + sc_ref[0]) + sh_ref[0]
    hb = h.astype(BF16)

    width = 2 * LANES
    slabs = ([(zr_ref, wr_ref, c, False) for c in range(0, zr_ref.shape[2], width)]
             + [(sg_ref, wg_ref, c, True) for c in range(0, sg_ref.shape[2], width)])
    per_head = -(-len(slabs) // (MLA_HEADS + 1))

    def project(stage):
        for out_ref, w_ref, c, gate in slabs[stage * per_head:(stage + 1) * per_head]:
            y = _dot(hb, w_ref[:, c:c + width])
            out_ref[0, :, c:c + width] = _sigmoid(y).astype(BF16) if gate else y

    zm = _dot(hb, wm_ref[...])
    project(0)
    _mla_heads(zm, pos_ref, freq_ref, gq_ref, wuq_ref, wuqr_ref, gkv_ref,
               wukv_ref, wvt_ref, qn_ref, qnr_ref, kn_ref, knr_ref, kshift_ref, q_ref, k_ref, vt_ref,
               after_head=lambda h: project(h + 1))


def _front(x, sh, sc, g, wr, wm, wg, pos, freq, gq, wuq, gkv, wukv, qn, kn, kshift):
    bsz, seq, d = x.shape
    tm = 512
    tok = lambda b, j: (b, j, 0)
    bmap = lambda b, j: (b, 0, 0)
    head = lambda b, j: (b, 0, j, 0)
    once = lambda a: pl.BlockSpec(a.shape, lambda b, j: (0,) * a.ndim, pipeline_mode=pl.Buffered(1))
    wuqr = _rot_half(wuq.reshape(Q_LORA, MLA_HEADS, HEAD_PAD)).reshape(Q_LORA, -1)
    unsign = _rot_half(jnp.ones((1, LANES), F32))
    qnr = _rot_half(qn) * unsign
    knr = _rot_half(kn) * unsign
    wv = wukv.reshape(KV_LORA, MLA_HEADS, HEAD_PAD)[:, :, QK_NOPE_DIM:]
    wvt = jnp.pad(wv.transpose(1, 2, 0), ((0, 0), (0, VT_ROWS - V_HEAD_DIM), (0, 0)))
    vec = _const_spec((1, LANES))
    return pl.pallas_call(
        _front_kernel,
        grid=(bsz, seq // tm),
        in_specs=[pl.BlockSpec((1, tm, d), tok), pl.BlockSpec((1, 1, d), bmap),
                  pl.BlockSpec((1, 1, d), bmap), _const_spec((1, d)), once(wr), once(wm), once(wg),
                  pl.BlockSpec((1, tm, 1), tok), vec, _const_spec((1, Q_LORA)), once(wuq),
                  once(wuqr), _const_spec((1, KV_LORA)), once(wukv), once(wvt),
                  vec, vec, vec, vec, vec],
        out_specs=[pl.BlockSpec((1, tm, wr.shape[1]), tok), pl.BlockSpec((1, tm, wg.shape[1]), tok),
                   pl.BlockSpec((1, MLA_HEADS, tm, HEAD_PAD), head),
                   pl.BlockSpec((1, MLA_HEADS, tm, HEAD_PAD), head),
                   pl.BlockSpec((1, MLA_HEADS, VT_ROWS, tm), lambda b, j: (b, 0, 0, j))],
        out_shape=[jax.ShapeDtypeStruct((bsz, seq, wr.shape[1]), F32),
                   jax.ShapeDtypeStruct((bsz, seq, wg.shape[1]), BF16),
                   jax.ShapeDtypeStruct((bsz, MLA_HEADS, seq, HEAD_PAD), BF16),
                   jax.ShapeDtypeStruct((bsz, MLA_HEADS, seq, HEAD_PAD), BF16),
                   jax.ShapeDtypeStruct((bsz, MLA_HEADS, VT_ROWS, seq), BF16)],
        compiler_params=_cparams("arbitrary", "arbitrary"),
        name="front",
    )(x, sh, sc, g, wr, wm, wg, pos, freq, gq, wuq, wuqr, gkv, wukv, wvt, qn, qnr, kn, knr, kshift)


def _attn_kernel(qi_ref, kj_ref, fixed_ref, q_ref, k_ref, vt_ref, o_ref, m_scr, acc_scr):
    p = pl.program_id(1)
    qi = qi_ref[p]
    kj = kj_ref[p]
    fixed_shift = fixed_ref[0] == 1
    bq = q_ref.shape[2]
    bk = k_ref.shape[2]
    heads = range(MLA_HEADS)

    @pl.when(kj == 0)
    def _():
        m_scr[...] = jnp.full(m_scr.shape, NEG_BIG, F32)
        acc_scr[...] = jnp.zeros(acc_scr.shape, F32)

    def scores(diagonal):
        st = [_dot_nt(k_ref[0, h], q_ref[0, h]) for h in heads]
        if diagonal:
            visible = (lax.broadcasted_iota(jnp.int32, (bk, bq), 0)
                       <= lax.broadcasted_iota(jnp.int32, (bk, bq), 1))
            st = [jnp.where(visible, x, NEG_BIG) for x in st]
        return st

    def block_fixed(diagonal):
        st = scores(diagonal)
        pr = [jnp.exp2(st[h]).astype(BF16) for h in heads]
        pv = [_dot(vt_ref[0, h], pr[h]) for h in heads]
        for h in heads:
            acc_scr[h] += pv[h]

    def block(diagonal):
        st = scores(diagonal)
        m_prev = [m_scr[h] for h in heads]
        m_new = [jnp.maximum(m_prev[h], jnp.max(st[h], axis=0, keepdims=True)) for h in heads]
        pr = [jnp.exp2(st[h] - m_new[h]).astype(BF16) for h in heads]
        alpha = [jnp.exp2(m_prev[h] - m_new[h]) for h in heads]
        pv = [_dot(vt_ref[0, h], pr[h]) for h in heads]
        for h in heads:
            acc_scr[h] = alpha[h] * acc_scr[h] + pv[h]
            m_scr[h] = m_new[h]

    for diagonal in (False, True):
        on_diag = (kj == qi) if diagonal else (kj != qi)
        pl.when(on_diag & fixed_shift)(functools.partial(block_fixed, diagonal))
        pl.when(on_diag & jnp.logical_not(fixed_shift))(functools.partial(block, diagonal))

    @pl.when(kj == qi)
    def _():
        for h in heads:
            acc = acc_scr[h]
            o = acc[:V_HEAD_DIM] * (1.0 / acc[V_HEAD_DIM:V_HEAD_DIM + 1])
            o_ref[0, h * V_HEAD_DIM:(h + 1) * V_HEAD_DIM, :] = o.astype(BF16)


def _attn(q, k, v, fixed):
    bsz, nh, seq, _ = q.shape
    bq = bk = 512
    nq = seq // bq
    qi = np.concatenate([np.full(i + 1, i) for i in range(nq)]).astype(np.int32)
    kj = np.concatenate([np.arange(i + 1) for i in range(nq)]).astype(np.int32)
    grid_spec = pltpu.PrefetchScalarGridSpec(
        num_scalar_prefetch=3,
        grid=(bsz, len(qi)),
        in_specs=[pl.BlockSpec((1, nh, bq, HEAD_PAD), lambda b, p, qi, kj, f: (b, 0, qi[p], 0)),
                  pl.BlockSpec((1, nh, bk, HEAD_PAD), lambda b, p, qi, kj, f: (b, 0, kj[p], 0)),
                  pl.BlockSpec((1, nh, VT_ROWS, bk), lambda b, p, qi, kj, f: (b, 0, 0, kj[p]))],
        out_specs=pl.BlockSpec((1, nh * V_HEAD_DIM, bq), lambda b, p, qi, kj, f: (b, 0, qi[p])),
        scratch_shapes=[pltpu.VMEM((nh, 1, bq), F32), pltpu.VMEM((nh, VT_ROWS, bq), F32)],
    )
    return pl.pallas_call(
        _attn_kernel,
        grid_spec=grid_spec,
        out_shape=jax.ShapeDtypeStruct((bsz, nh * V_HEAD_DIM, seq), BF16),
        compiler_params=_cparams("arbitrary", "arbitrary"),
        name="mla_attn",
    )(jnp.asarray(qi), jnp.asarray(kj), fixed, q, k, v)


def _first_index_of_max(vals, lane):
    mx = jnp.max(vals, axis=-1, keepdims=True)
    idx = jnp.min(jnp.where(vals == mx, lane, LANES), axis=-1, keepdims=True)
    return mx, idx


def _post_kernel(yr_ref, ymt_ref, sg_ref, x_ref, gt_ref, sh_ref, sc_ref, g_ref,
                 wb0_ref, wb1_ref, wo_ref, wr_ref, br_ref, x1_ref, h2_ref, cnt_ref):
    sg = sg_ref[...].astype(F32)
    merged = (sg[:, :D_MODEL] * _dot(yr_ref[...], wb0_ref[...])
              + sg[:, D_MODEL:] * _dot_tn(ymt_ref[0], wb1_ref[...]))
    x1 = x_ref[...] + gt_ref[0] * _mm(merged, wo_ref[...])
    x1_ref[...] = x1
    ms = jnp.mean(x1 * x1, axis=-1, keepdims=True)
    h2 = x1 * lax.rsqrt(ms + EPS) * g_ref[...]
    h2 = h2 * (1.0 + sc_ref[0]) + sh_ref[0]
    h2_ref[:, :D_MODEL] = h2.astype(BF16)

    logits = _mm3(h2, wr_ref[...]) + br_ref[...]
    lane = lax.broadcasted_iota(jnp.int32, logits.shape, 1).astype(F32)
    gl = jnp.where((lane >= N_EXPERTS) & (lane < N_EXPERTS + N_GROUPS), logits, NEG_BIG)
    gmax, gidx = _first_index_of_max(gl, lane)
    p_g = 1.0 / jnp.sum(jnp.exp(gl - gmax), axis=-1, keepdims=True)
    first = (gidx - N_EXPERTS) * EXPERTS_PER_GROUP
    el = jnp.where((lane >= first) & (lane < first + EXPERTS_PER_GROUP), logits, NEG_BIG)
    m1, i1 = _first_index_of_max(el, lane)
    z = jnp.sum(jnp.exp(el - m1), axis=-1, keepdims=True)
    el2 = jnp.where(lane == i1, NEG_BIG, el)
    m2, i2 = _first_index_of_max(el2, lane)
    p1 = 1.0 / z
    p2 = jnp.exp(m2 - m1) / z
    tot = p1 + p2
    group_onehot = jnp.where(lane == gidx, 1.0, 0.0)
    cmb = (jnp.where(lane == i1, p1 / tot * p_g, 0.0)
           + jnp.where(lane == i2, p2 / tot * p_g, 0.0) + group_onehot)
    cmb_hi, cmb_lo = _split(cmb)
    h2_ref[:, D_MODEL:D_MODEL + LANES] = cmb_hi
    h2_ref[:, D_MODEL + LANES:] = cmb_lo
    cnt_ref[0] = jnp.broadcast_to(jnp.sum(group_onehot, axis=0, keepdims=True), cnt_ref.shape[1:])


def _post(yr, ym, sg, x2, gt, sh, sc, g, wb0, wb1, wo, wr, br, seq):
    n, d = x2.shape
    tm = 512
    per_b = seq // tm
    bmap = lambda i: (i // per_b, 0, 0)
    rowmap = lambda i: (i, 0)
    return pl.pallas_call(
        _post_kernel,
        grid=(n // tm,),
        in_specs=[pl.BlockSpec((tm, D_RWKV), rowmap),
                  pl.BlockSpec((1, D_MLA, tm), lambda i: (i // per_b, 0, i % per_b)),
                  pl.BlockSpec((tm, 2 * d), rowmap), pl.BlockSpec((tm, d), rowmap),
                  pl.BlockSpec((1, 1, d), bmap), pl.BlockSpec((1, 1, d), bmap),
                  pl.BlockSpec((1, 1, d), bmap), _const_spec((1, d)),
                  _const_spec(wb0.shape), _const_spec(wb1.shape), _const_spec(wo.shape),
                  _const_spec(wr.shape), _const_spec(br.shape)],
        out_specs=[pl.BlockSpec((tm, d), rowmap), pl.BlockSpec((tm, d + 2 * LANES), rowmap),
                   pl.BlockSpec((1, 8, LANES), lambda i: (i, 0, 0))],
        out_shape=[jax.ShapeDtypeStruct((n, d), F32),
                   jax.ShapeDtypeStruct((n, d + 2 * LANES), BF16),
                   jax.ShapeDtypeStruct((n // tm, 8, LANES), F32)],
        compiler_params=_cparams("arbitrary"),
        name="post",
    )(yr, ym, sg, x2, gt, sh, sc, g, wb0, wb1, wo, wr, br)


MOE_TILE = 1024
MOE_CHUNK = 256


def _moe_kernel(cnt_ref, h_ref, x1_ref, gt_ref, wg_ref, wu_ref, wd_ref, o_ref,
                key_scr):
    i = pl.program_id(0)
    g = pl.program_id(1)
    tm = h_ref.shape[0]
    d = o_ref.shape[1]
    lane = lax.broadcasted_iota(jnp.int32, (1, LANES), 1)

    @pl.when(g == 0)
    def _():
        o_ref[...] = x1_ref[...]
        onehot = jnp.where((lane >= N_EXPERTS) & (lane < N_EXPERTS + N_GROUPS),
                           h_ref[:, d:d + LANES].astype(F32), 0.0)
        before = (lax.broadcasted_iota(jnp.int32, (tm, tm), 1)
                  < lax.broadcasted_iota(jnp.int32, (tm, tm), 0))
        rank = _dot(jnp.where(before, 1.0, 0.0).astype(BF16), onehot.astype(BF16))
        key = jnp.where(onehot > 0.5, rank, -1.0)
        key_scr[...] = key.T

    key_row = key_scr[pl.ds(N_EXPERTS + g, 1), :]
    h = h_ref[...]
    gt = gt_ref[0]
    first_lane = g * EXPERTS_PER_GROUP

    def run_chunk(first_slot, rows):
        slot = (first_slot + lax.broadcasted_iota(jnp.int32, (rows, 1), 0)).astype(F32)
        sel = jnp.where(key_row == slot, 1.0, 0.0).astype(BF16)
        gathered = _dot(sel, h)
        xg = gathered[:, :d].astype(BF16)
        cw = gathered[:, d:d + LANES] + gathered[:, d + LANES:]
        y = jnp.zeros((rows, d), F32)
        for e in range(EXPERTS_PER_GROUP):
            gate = _dot(xg, wg_ref[e])
            hid = gate * _sigmoid(gate) * _dot(xg, wu_ref[e])
            cwe = jnp.sum(jnp.where(lane == first_lane + e, cw, 0.0), axis=-1, keepdims=True)
            y = y + _mm(hid * cwe, wd_ref[e])
        o_ref[...] += _dot_tn(sel, (y * gt).astype(BF16))

    cnt = cnt_ref[i * N_GROUPS + g]
    n_full = cnt // MOE_CHUNK
    rem = cnt - n_full * MOE_CHUNK
    n_full = n_full + (rem > MOE_CHUNK // 2).astype(jnp.int32)

    def full(c, carry):
        run_chunk(c * MOE_CHUNK, MOE_CHUNK)
        return carry

    lax.fori_loop(0, n_full, full, 0)

    @pl.when((rem > 0) & (rem <= MOE_CHUNK // 2))
    def _():
        run_chunk(n_full * MOE_CHUNK, MOE_CHUNK // 2)


def _moe(cnt, h2, x1, gt, wg, wu, wd, seq):
    n, d = x1.shape
    tm = MOE_TILE
    per_b = seq // tm
    rowmap = lambda i, g, cnt: (i, 0)
    wmap = lambda i, g, cnt: (g, 0, 0)
    grid_spec = pltpu.PrefetchScalarGridSpec(
        num_scalar_prefetch=1,
        grid=(n // tm, N_GROUPS),
        in_specs=[pl.BlockSpec((tm, h2.shape[1]), rowmap),
                  pl.BlockSpec((tm, d), rowmap),
                  pl.BlockSpec((1, 1, d), lambda i, g, cnt: (i // per_b, 0, 0)),
                  pl.BlockSpec((EXPERTS_PER_GROUP, d, D_EXPERT), wmap),
                  pl.BlockSpec((EXPERTS_PER_GROUP, d, D_EXPERT), wmap),
                  pl.BlockSpec((EXPERTS_PER_GROUP, D_EXPERT, d), wmap)],
        out_specs=pl.BlockSpec((tm, d), rowmap),
        scratch_shapes=[pltpu.VMEM((LANES, tm), F32)],
    )
    return pl.pallas_call(
        _moe_kernel,
        grid_spec=grid_spec,
        out_shape=jax.ShapeDtypeStruct((n, d), F32),
        compiler_params=_cparams("arbitrary", "arbitrary"),
        name="moe",
    )(cnt, h2, x1, gt, wg, wu, wd)


def _pad_cols(w, n):
    return jnp.pad(w, ((0, 0), (0, n - w.shape[1])))


def kernel(x, c, positions, w_ada, b_ada, g_norm_mix, w_in, mu_shift, w0, w_decay_up, a0, w_a_up, w_g_up, k_k, k_a, r_k, ln_x_w, ln_x_b, g_q_a, w_uq, g_kv_a, w_ukv, q_norm, k_norm, w_branch, w_out, g_norm_ffn, w_router_group, b_router_group, w_router_expert, b_router_expert, w_e_gate, w_e_up, w_e_down):
    bsz, seq, d = x.shape
    n = bsz * seq
    row = lambda a: a.reshape(1, -1)

    freqs = ROPE_THETA ** (-(jnp.arange(0, QK_ROPE_DIM, 2, dtype=F32) / QK_ROPE_DIM))
    freq = jnp.zeros((1, LANES), F32).at[0, QK_NOPE_DIM:QK_NOPE_DIM + QK_ROPE_DIM].set(
        jnp.concatenate([freqs, freqs]))
    pos = positions.astype(F32).reshape(bsz, seq, 1)

    x2 = x.reshape(n, d)
    for l in range(w_ada.shape[0]):
        mod = _ada(c, w_ada[l], b_ada[l])
        sh_m, sc_m, gt_m, sh_f, sc_f, gt_f = [m.reshape(bsz, 1, d) for m in jnp.split(mod, 6, axis=-1)]

        w_in_l = w_in[l].astype(BF16)
        wr = w_in_l[:, :RWKV_COLS]
        wm = _pad_cols(w_in_l[:, RWKV_COLS:RWKV_COLS + MLA_COLS], MLA_COLS_PAD)
        wg = w_in_l[:, RWKV_COLS + MLA_COLS:]
        wuq = w_uq[l].reshape(Q_LORA, MLA_HEADS, QK_HEAD_DIM)
        wuq = jnp.pad(wuq, ((0, 0), (0, 0), (0, HEAD_PAD - QK_HEAD_DIM))).reshape(Q_LORA, -1)
        qn = _pad_cols(row(q_norm[l]), LANES)
        kn = _pad_cols(row(k_norm[l]), LANES)
        bound = (QK_HEAD_DIM * jnp.max(jnp.abs(q_norm[l])) * jnp.max(jnp.abs(k_norm[l]))
                 * QK_HEAD_DIM ** -0.5 * LOG2_E)
        fixed = 2.0 * bound <= MAX_FIXED_SHIFT_SPAN
        kshift = jnp.zeros((1, LANES), F32).at[0, QK_HEAD_DIM].set(jnp.where(fixed, -bound, 0.0))
        zr, sg, q, k, v = _front(x2.reshape(bsz, seq, d), sh_m, sc_m, row(g_norm_mix[l]), wr, wm, wg,
                                 pos, freq, row(g_q_a[l]), wuq.astype(BF16), row(g_kv_a[l]),
                                 w_ukv[l].astype(BF16), qn, kn, kshift)
        sg = sg.reshape(n, -1)

        zeros_lora = jnp.zeros((DECAY_LORA, D_RWKV), F32)
        wdec = jnp.concatenate([w_decay_up[l], zeros_lora], axis=0)
        wa = jnp.concatenate([zeros_lora, w_a_up[l]], axis=0)
        prep = _rwkv_prep(zr.reshape(bsz, seq, RWKV_COLS), row(mu_shift[l]), row(w0[l]), wdec,
                          row(a0[l]), wa, w_g_up[l], row(k_k[l]), row(k_a[l]), row(r_k[l]))
        y_rwkv = _rwkv_scan(*prep, row(ln_x_w[l]), row(ln_x_b[l]))

        y_mla = _attn(q, k, v, fixed.astype(jnp.int32).reshape(1))

        w_router = _pad_cols(jnp.concatenate([w_router_expert[l], w_router_group[l]], axis=1), LANES)
        b_router = _pad_cols(row(jnp.concatenate([b_router_expert[l], b_router_group[l]])), LANES)
        x1, h2, cnt = _post(y_rwkv.reshape(n, D_RWKV), y_mla, sg, x2,
                            gt_m, sh_f, sc_f, row(g_norm_ffn[l]),
                            w_branch[l, 0].astype(BF16), w_branch[l, 1].astype(BF16),
                            w_out[l].astype(BF16), w_router, b_router, seq)

        cnt = cnt[:, 0, N_EXPERTS:N_EXPERTS + N_GROUPS].reshape(n // MOE_TILE, -1, N_GROUPS)
        cnt = jnp.sum(cnt, axis=1).astype(jnp.int32).reshape(-1)
        x2 = _moe(cnt, h2, x1, gt_f, w_e_gate[l].astype(BF16), w_e_up[l].astype(BF16),
                  w_e_down[l].astype(BF16), seq)
    return x2.reshape(bsz, seq, d)
```
